```python
import math
import jax, jax.numpy as jnp
from jax import lax
import numpy as np

D_MODEL = 2048
BATCH = 2
SEQ = 4096
DEPTH = 4
DEC_BATCH = 8
DEC_SEQ = 1
PAST_LEN = 16384
PAGE_SIZE = 128

N_A_LAYERS = DEPTH // 2
N_B_LAYERS = DEPTH - N_A_LAYERS
CONV_WIDTH = 31
D_FF = 4 * D_MODEL
HEAD_DIM = 128
V_DIM = 2 * HEAD_DIM
N_HEADS = D_MODEL // V_DIM
ROT_DIM = HEAD_DIM // 4
ROPE_THETA = 500000.0
Q_BLOCK = 128
EPS = 1e-6
SCALE = HEAD_DIM ** -0.5
POOL_NUM = 5
POOL_DEN = 4

kernel_name = "yoco_conformer_diffattn_step"


def _rmsnorm(x, g):
    xf = x.astype(jnp.float32)
    y = xf * lax.rsqrt(jnp.mean(xf * xf, axis=-1, keepdims=True) + EPS)
    return (y * g.astype(jnp.float32)).astype(x.dtype)


def _layernorm(x, g, b):
    xf = x.astype(jnp.float32)
    mu = jnp.mean(xf, axis=-1, keepdims=True)
    var = jnp.mean(jnp.square(xf - mu), axis=-1, keepdims=True)
    y = (xf - mu) * lax.rsqrt(var + EPS)
    return (y * g.astype(jnp.float32) + b.astype(jnp.float32)).astype(x.dtype)


def _adaln(c, w, b, n):
    m = jax.nn.silu(c) @ w + b
    return [t[:, None, :] for t in jnp.split(m, n, axis=-1)]


def _rope_partial(x, pos):
    half = ROT_DIM // 2
    inv_freq = jnp.power(ROPE_THETA, -jnp.arange(half, dtype=jnp.float32) * 2.0 / ROT_DIM)
    ang = pos.astype(jnp.float32)[:, None] * inv_freq[None, :]
    cos = jnp.cos(ang)[None, :, None, None, :]
    sin = jnp.sin(ang)[None, :, None, None, :]
    xf = x.astype(jnp.float32)
    x1, x2, xp = xf[..., :half], xf[..., half:ROT_DIM], xf[..., ROT_DIM:]
    out = jnp.concatenate([x1 * cos - x2 * sin, x2 * cos + x1 * sin, xp], axis=-1)
    return out.astype(x.dtype)


def _conv_module(h, conv_state, w_pw1, b_pw1, w_dw, b_dw, ln_g, ln_b, w_pw2, b_pw2):
    u = h @ w_pw1 + b_pw1
    a, gt = jnp.split(u, 2, axis=-1)
    glu = a * jax.nn.sigmoid(gt)
    full = jnp.concatenate([conv_state.astype(glu.dtype), glu], axis=1)
    y = lax.conv_general_dilated(
        full, w_dw[:, None, :].astype(glu.dtype), window_strides=(1,), padding="VALID",
        dimension_numbers=("NWC", "WIO", "NWC"), feature_group_count=glu.shape[-1]) + b_dw
    y = jax.nn.silu(_layernorm(y, ln_g, ln_b))
    return y @ w_pw2 + b_pw2, full[:, -(CONV_WIDTH - 1):]


def _sqrelu_mlp(h, w1, w2):
    return jnp.square(jax.nn.relu(h @ w1)) @ w2


def _diff_weights(s, lam):
    p = jax.nn.softmax(s, axis=-1)
    return p[:, :, 0] - lam * p[:, :, 1]


def _diff_attn_prompt(q, k, v, lam):
    b, s = q.shape[0], q.shape[1]
    nb = s // Q_BLOCK
    qb = q.reshape(b, nb, Q_BLOCK, N_HEADS, 2, HEAD_DIM).swapaxes(0, 1)
    kpos = jnp.arange(s)

    def block(args):
        qi, bi = args
        sc = jnp.einsum("bqhcd,bkhcd->bhcqk", qi, k, preferred_element_type=jnp.float32) * SCALE
        qpos = bi * Q_BLOCK + jnp.arange(Q_BLOCK)
        sc = jnp.where(kpos[None, :] <= qpos[:, None], sc, -jnp.inf)
        w = _diff_weights(sc, lam)
        return jnp.einsum("bhqk,bkhe->bqhe", w.astype(v.dtype), v)

    o = lax.map(block, (qb, jnp.arange(nb)))
    return o.swapaxes(0, 1).reshape(b, s, N_HEADS, V_DIM)


def _diff_attn_sample(q, k_past, v_past, k_new, v_new, lam):
    t = q.shape[1]
    p_len = k_past.shape[1]
    s_past = jnp.einsum("bqhcd,bphcd->bhcqp", q, k_past, preferred_element_type=jnp.float32) * SCALE
    s_new = jnp.einsum("bqhcd,bthcd->bhcqt", q, k_new, preferred_element_type=jnp.float32) * SCALE
    causal = jnp.tril(jnp.ones((t, t), dtype=bool))
    s_new = jnp.where(causal, s_new, -jnp.inf)
    w = _diff_weights(jnp.concatenate([s_past, s_new], axis=-1), lam)
    return (jnp.einsum("bhqp,bphe->bqhe", w[..., :p_len].astype(v_past.dtype), v_past)
            + jnp.einsum("bhqt,bthe->bqhe", w[..., p_len:].astype(v_new.dtype), v_new))


def _shared_kv(x, c, pos, kv_ada_w, kv_ada_b, kv_norm_g, w_k, w_v, k_norm_g):
    b, s = x.shape[0], x.shape[1]
    sh, sc = _adaln(c, kv_ada_w, kv_ada_b, 2)
    h = _rmsnorm(x, kv_norm_g) * (1 + sc) + sh
    k = _rope_partial(_rmsnorm((h @ w_k).reshape(b, s, N_HEADS, 2, HEAD_DIM), k_norm_g), pos)
    v = (h @ w_v).reshape(b, s, N_HEADS, V_DIM)
    return k, v


def _trunk(x, c, pos, conv_state, attend,
           ada_w, ada_b, norm_mix_g, norm_ff_g, w_ff1, w_ff2,
           conv_w_pw1, conv_b_pw1, conv_w_dw, conv_b_dw, conv_ln_g, conv_ln_b, conv_w_pw2, conv_b_pw2,
           kv_ada_w, kv_ada_b, kv_norm_g, w_k, w_v, k_norm_g,
           w_q, q_norm_g, lambda_q1, lambda_k1, lambda_q2, lambda_k2, subln_g, w_o):
    b, s = x.shape[0], x.shape[1]
    new_conv = []
    k = v = None
    for l in range(DEPTH):
        if l == N_A_LAYERS:
            k, v = _shared_kv(x, c, pos, kv_ada_w, kv_ada_b, kv_norm_g, w_k, w_v, k_norm_g)
        sh_m, sc_m, g_m, sh_f, sc_f, g_f = _adaln(c, ada_w[l], ada_b[l], 6)
        h = _rmsnorm(x, norm_mix_g[l]) * (1 + sc_m) + sh_m
        if l < N_A_LAYERS:
            out, st = _conv_module(h, conv_state[l], conv_w_pw1[l], conv_b_pw1[l], conv_w_dw[l],
                                   conv_b_dw[l], conv_ln_g[l], conv_ln_b[l], conv_w_pw2[l], conv_b_pw2[l])
            new_conv.append(st)
        else:
            j = l - N_A_LAYERS
            lam_init = 0.8 - 0.6 * math.exp(-0.3 * l)
            q = (h @ w_q[j]).reshape(b, s, N_HEADS, 2, HEAD_DIM)
            q = _rope_partial(_rmsnorm(q, q_norm_g[j]), pos)
            lam = (jnp.exp(jnp.sum(lambda_q1[j].astype(jnp.float32) * lambda_k1[j].astype(jnp.float32)))
                   - jnp.exp(jnp.sum(lambda_q2[j].astype(jnp.float32) * lambda_k2[j].astype(jnp.float32)))
                   + lam_init)
            o = attend(q, k, v, lam)
            o = _rmsnorm(o, subln_g[j]) * (1.0 - lam_init)
            out = o.reshape(b, s, D_MODEL) @ w_o[j]
        x = x + g_m * out
        h = _rmsnorm(x, norm_ff_g[l]) * (1 + sc_f) + sh_f
        x = x + g_f * _sqrelu_mlp(h, w_ff1[l], w_ff2[l])
    return x, jnp.stack(new_conv), k, v


def setup_inputs(seed: int = 0) -> dict:
    key = jax.random.key(seed)
    ks = iter(jax.random.split(key, 64))
    f32 = jnp.float32
    D = D_MODEL

    def normal(shape, std):
        return jax.random.normal(next(ks), shape, f32) * std

    def gain(shape):
        return 1.0 + normal(shape, 0.02)

    n_pages = PAST_LEN // PAGE_SIZE
    n_pool = (DEC_BATCH * n_pages * POOL_NUM) // POOL_DEN
    page_table = jax.random.permutation(next(ks), n_pool)[: DEC_BATCH * n_pages]
    page_table = page_table.reshape(DEC_BATCH, n_pages).astype(jnp.int32)
    return {
        "x_prompt": normal((BATCH, SEQ, D), 1.0),
        "x_sample": normal((DEC_BATCH, DEC_SEQ, D), 1.0),
        "state_conv": normal((N_A_LAYERS, DEC_BATCH, CONV_WIDTH - 1, D), 0.5),
        "cache_k": normal((n_pool, PAGE_SIZE, N_HEADS, 2, HEAD_DIM), 1.0),
        "cache_v": normal((n_pool, PAGE_SIZE, N_HEADS, V_DIM), 1.0),
        "page_table": page_table,
        "c_prompt": normal((BATCH, D), 1.0),
        "c_sample": normal((DEC_BATCH, D), 1.0),
        "ada_w": normal((DEPTH, D, 6 * D), 0.5 * D ** -0.5),
        "ada_b": normal((DEPTH, 6 * D), 0.02),
        "norm_mix_g": gain((DEPTH, D)),
        "norm_ff_g": gain((DEPTH, D)),
        "w_ff1": normal((DEPTH, D, D_FF), D ** -0.5),
        "w_ff2": normal((DEPTH, D_FF, D), D_FF ** -0.5),
        "conv_w_pw1": normal((N_A_LAYERS, D, 2 * D), D ** -0.5),
        "conv_b_pw1": normal((N_A_LAYERS, 2 * D), 0.02),
        "conv_w_dw": normal((N_A_LAYERS, CONV_WIDTH, D), CONV_WIDTH ** -0.5),
        "conv_b_dw": normal((N_A_LAYERS, D), 0.02),
        "conv_ln_g": gain((N_A_LAYERS, D)),
        "conv_ln_b": normal((N_A_LAYERS, D), 0.02),
        "conv_w_pw2": normal((N_A_LAYERS, D, D), D ** -0.5),
        "conv_b_pw2": normal((N_A_LAYERS, D), 0.02),
        "kv_ada_w": normal((D, 2 * D), 0.5 * D ** -0.5),
        "kv_ada_b": normal((2 * D,), 0.02),
        "kv_norm_g": gain((D,)),
        "w_k": normal((D, N_HEADS * 2 * HEAD_DIM), D ** -0.5),
        "w_v": normal((D, N_HEADS * V_DIM), D ** -0.5),
        "k_norm_g": gain((HEAD_DIM,)),
        "w_q": normal((N_B_LAYERS, D, N_HEADS * 2 * HEAD_DIM), D ** -0.5),
        "q_norm_g": gain((N_B_LAYERS, HEAD_DIM)),
        "lambda_q1": normal((N_B_LAYERS, HEAD_DIM), 0.1),
        "lambda_k1": normal((N_B_LAYERS, HEAD_DIM), 0.1),
        "lambda_q2": normal((N_B_LAYERS, HEAD_DIM), 0.1),
        "lambda_k2": normal((N_B_LAYERS, HEAD_DIM), 0.1),
        "subln_g": gain((N_B_LAYERS, V_DIM)),
        "w_o": normal((N_B_LAYERS, N_HEADS * V_DIM, D), D ** -0.5),
    }


def reference(x_prompt, x_sample, state_conv, cache_k, cache_v, page_table, c_prompt, c_sample,
              ada_w, ada_b, norm_mix_g, norm_ff_g, w_ff1, w_ff2,
              conv_w_pw1, conv_b_pw1, conv_w_dw, conv_b_dw, conv_ln_g, conv_ln_b, conv_w_pw2, conv_b_pw2,
              kv_ada_w, kv_ada_b, kv_norm_g, w_k, w_v, k_norm_g,
              w_q, q_norm_g, lambda_q1, lambda_k1, lambda_q2, lambda_k2, subln_g, w_o):
    weights = (ada_w, ada_b, norm_mix_g, norm_ff_g, w_ff1, w_ff2,
               conv_w_pw1, conv_b_pw1, conv_w_dw, conv_b_dw, conv_ln_g, conv_ln_b, conv_w_pw2, conv_b_pw2,
               kv_ada_w, kv_ada_b, kv_norm_g, w_k, w_v, k_norm_g,
               w_q, q_norm_g, lambda_q1, lambda_k1, lambda_q2, lambda_k2, subln_g, w_o)
    dec_b, n_pages = page_table.shape
    n_past = n_pages * PAGE_SIZE

    pos_p = jnp.arange(x_prompt.shape[1])
    conv0 = jnp.zeros((N_A_LAYERS, x_prompt.shape[0], CONV_WIDTH - 1, D_MODEL), x_prompt.dtype)
    y_prompt, conv_p, k_p, v_p = _trunk(x_prompt, c_prompt, pos_p, conv0, _diff_attn_prompt, *weights)

    pos_s = n_past + jnp.arange(x_sample.shape[1])
    k_past = cache_k[page_table].reshape(dec_b, n_past, N_HEADS, 2, HEAD_DIM)
    v_past = cache_v[page_table].reshape(dec_b, n_past, N_HEADS, V_DIM)

    def attend_sample(q, k_new, v_new, lam):
        return _diff_attn_sample(q, k_past, v_past, k_new, v_new, lam)

    y_sample, conv_s, k_s, v_s = _trunk(x_sample, c_sample, pos_s, state_conv, attend_sample, *weights)
    return (y_prompt, y_sample, conv_p, conv_s, k_p, v_p, k_s, v_s)
```

```python
import functools
import math
from typing import NamedTuple

import jax
import jax.numpy as jnp
from jax import lax
from jax.experimental import pallas as pl
from jax.experimental.pallas import tpu as pltpu

F32 = jnp.float32
BF16 = jnp.bfloat16

D_MODEL = 2048
DEPTH = 4
N_A_LAYERS = DEPTH // 2
CONV_WIDTH = 31
HEAD_DIM = 128
V_DIM = 2 * HEAD_DIM
N_HEADS = D_MODEL // V_DIM
ROT_DIM = HEAD_DIM // 4
ROPE_THETA = 500000.0
EPS = 1e-6
SCALE = HEAD_DIM ** -0.5
PAGE_SIZE = 128
NEG_BIG = -1e30

LANE = 128
SUBLANE = 8
VMEM_LIMIT_BYTES = 56 * 1024 * 1024

SAMPLE_ROWS = 16
TM_PROMPT = 512
TN_PROJ = 1024
TN_GLU = 512
TF_MLP = 512
TN_ADA = 1024
NORM_ROWS = 64
CONV_COLS = 256
CONV_ROWS = 64
CONV_HALO = 32
TQ_ATTN = 512
PAGES_PER_STEP = 8


class Rows(NamedTuple):
    m: int
    tm: int
    tiles_per_seq: int
    per_row: bool


def _cparams(*sem):
    return pltpu.CompilerParams(dimension_semantics=sem, vmem_limit_bytes=VMEM_LIMIT_BYTES)


def _mod_spec(rows, mods, k, width, col_of_j):
    nb = D_MODEL // width
    if rows.per_row:
        return pl.BlockSpec((rows.tm, width), lambda i, j: (0, k * nb + col_of_j(j)))
    return pl.BlockSpec((None, 1, width), lambda i, j: (i // rows.tiles_per_seq, 0, k * nb + col_of_j(j)))


def _rows_of(ref, r0, n):
    if ref.shape[0] == 1:
        return ref[...]
    return ref[pl.ds(r0, n), :]


def _norm_mod_to_bf16(x_ref, g_ref, sc_ref, sh_ref, h_ref):
    tm = x_ref.shape[0]
    rb = min(tm, NORM_ROWS)

    def body(r, carry):
        r0 = pl.multiple_of(r * rb, rb)
        x = x_ref[pl.ds(r0, rb), :]
        y = x * lax.rsqrt(jnp.mean(x * x, axis=-1, keepdims=True) + EPS)
        y = y * g_ref[...]
        h = y * (1.0 + _rows_of(sc_ref, r0, rb)) + _rows_of(sh_ref, r0, rb)
        h_ref[pl.ds(r0, rb), :] = h.astype(BF16)
        return carry

    lax.fori_loop(0, tm // rb, body, 0)


def _dot(a, b):
    return jnp.dot(a, b, preferred_element_type=F32)


def _ada_kernel(c_ref, w_ref, b_ref, o_ref):
    c = c_ref[...]
    s = (c * jax.nn.sigmoid(c)).astype(BF16)
    o_ref[...] = _dot(s, w_ref[...].astype(BF16)) + b_ref[...]


def _ada(c_all, w, b):
    n_layers, _, n = w.shape
    r = c_all.shape[0]
    return pl.pallas_call(
        _ada_kernel,
        out_shape=jax.ShapeDtypeStruct((n_layers, r, n), F32),
        grid=(n_layers, n // TN_ADA),
        in_specs=[
            pl.BlockSpec((r, D_MODEL), lambda l, j: (0, 0)),
            pl.BlockSpec((None, D_MODEL, TN_ADA), lambda l, j: (l, 0, j)),
            pl.BlockSpec((None, 1, TN_ADA), lambda l, j: (l, 0, j)),
        ],
        out_specs=pl.BlockSpec((None, r, TN_ADA), lambda l, j: (l, 0, j)),
        compiler_params=_cparams("parallel", "parallel"),
        name="ada_mod",
    )(c_all, w, b.reshape(n_layers, 1, n))


def _pw1_glu_kernel(x_ref, g_ref, sc_ref, sh_ref, wa_ref, wg_ref, ba_ref, bg_ref, o_ref, h_ref):
    @pl.when(pl.program_id(1) == 0)
    def _():
        _norm_mod_to_bf16(x_ref, g_ref, sc_ref, sh_ref, h_ref)

    h = h_ref[...]
    a = _dot(h, wa_ref[...].astype(BF16)) + ba_ref[...]
    gt = _dot(h, wg_ref[...].astype(BF16)) + bg_ref[...]
    o_ref[...] = a * jax.nn.sigmoid(gt)


def _pw1_glu(rows, x, mods, norm_g, w, b):
    nj = D_MODEL // TN_GLU
    b2 = b.reshape(1, 2 * D_MODEL)
    full = lambda j: 0
    return pl.pallas_call(
        _pw1_glu_kernel,
        out_shape=jax.ShapeDtypeStruct((rows.m, D_MODEL), F32),
        grid=(rows.m // rows.tm, nj),
        in_specs=[
            pl.BlockSpec((rows.tm, D_MODEL), lambda i, j: (i, 0)),
            pl.BlockSpec((1, D_MODEL), lambda i, j: (0, 0)),
            _mod_spec(rows, mods, 1, D_MODEL, full),
            _mod_spec(rows, mods, 0, D_MODEL, full),
            pl.BlockSpec((D_MODEL, TN_GLU), lambda i, j: (0, j)),
            pl.BlockSpec((D_MODEL, TN_GLU), lambda i, j: (0, j + nj)),
            pl.BlockSpec((1, TN_GLU), lambda i, j: (0, j)),
            pl.BlockSpec((1, TN_GLU), lambda i, j: (0, j + nj)),
        ],
        out_specs=pl.BlockSpec((rows.tm, TN_GLU), lambda i, j: (i, j)),
        scratch_shapes=[pltpu.VMEM((rows.tm, D_MODEL), BF16)],
        compiler_params=_cparams("parallel", "arbitrary"),
        name="pw1_glu",
    )(x, norm_g.reshape(1, D_MODEL), mods, mods, w, w, b2, b2)


def _ln_silu_to_bf16(yc_ref, lng_ref, lnb_ref, y_ref):
    ncb, tm, cw = yc_ref.shape
    rb = min(tm, NORM_ROWS)

    def body(r, carry):
        r0 = pl.multiple_of(r * rb, rb)
        parts = [yc_ref[cb, pl.ds(r0, rb), :] for cb in range(ncb)]
        mu = sum(jnp.sum(p, axis=-1, keepdims=True) for p in parts) * (1.0 / D_MODEL)
        var = sum(jnp.sum(jnp.square(p - mu), axis=-1, keepdims=True) for p in parts) * (1.0 / D_MODEL)
        inv = lax.rsqrt(var + EPS)
        for cb in range(ncb):
            cols = slice(cb * cw, (cb + 1) * cw)
            y = (parts[cb] - mu) * inv * lng_ref[:, cols] + lnb_ref[:, cols]
            y = y * jax.nn.sigmoid(y)
            y_ref[pl.ds(r0, rb), cols] = y.astype(BF16)
        return carry

    lax.fori_loop(0, tm // rb, body, 0)


def _pw2_residual(y_ref, w_ref, b_ref, x_ref, gate_ref, o_ref):
    out = _dot(y_ref[...], w_ref[...].astype(BF16)) + b_ref[...]
    o_ref[...] = x_ref[...] + gate_ref[...] * out


def _conv_prompt_kernel(glu_ref, halo_ref, st_ref, wd_ref, bd_ref, lng_ref, lnb_ref,
                        w_ref, b_ref, x_ref, gate_ref, o_ref, win_ref, yc_ref, y_ref, *, tiles_per_seq):
    ncb, _, cw = win_ref.shape
    tm = glu_ref.shape[0]

    @pl.when(pl.program_id(1) == 0)
    def _():
        first = pl.program_id(0) % tiles_per_seq == 0
        for cb in range(ncb):
            cols = slice(cb * cw, (cb + 1) * cw)
            win_ref[cb, 0:CONV_HALO, :] = jnp.where(first, st_ref[:, cols], halo_ref[:, cols])
            win_ref[cb, CONV_HALO:, :] = glu_ref[:, cols]

        base = CONV_HALO - (CONV_WIDTH - 1)

        def conv_cols(cb, carry):
            wd = wd_ref[cb]
            for rb in range(tm // CONV_ROWS):
                acc = jnp.zeros((CONV_ROWS, cw), F32)
                for w in range(CONV_WIDTH):
                    acc = acc + win_ref[cb, pl.ds(rb * CONV_ROWS + base + w, CONV_ROWS), :] * wd[w:w + 1, :]
                yc_ref[cb, pl.ds(rb * CONV_ROWS, CONV_ROWS), :] = acc + bd_ref[cb]
            return carry

        lax.fori_loop(0, ncb, conv_cols, 0)
        _ln_silu_to_bf16(yc_ref, lng_ref, lnb_ref, y_ref)

    _pw2_residual(y_ref, w_ref, b_ref, x_ref, gate_ref, o_ref)


def _conv_sample_kernel(full_ref, wd_ref, bd_ref, lng_ref, lnb_ref,
                        w_ref, b_ref, x_ref, gate_ref, o_ref, yc_ref, y_ref):
    ncb, _, cw = yc_ref.shape

    @pl.when(pl.program_id(1) == 0)
    def _():
        for cb in range(ncb):
            cols = slice(cb * cw, (cb + 1) * cw)
            wd = wd_ref[cb]
            acc = jnp.zeros((full_ref.shape[1], cw), F32)
            for w in range(CONV_WIDTH):
                acc = acc + full_ref[w, :, cols] * wd[w:w + 1, :]
            yc_ref[cb] = acc + bd_ref[cb]
        _ln_silu_to_bf16(yc_ref, lng_ref, lnb_ref, y_ref)

    _pw2_residual(y_ref, w_ref, b_ref, x_ref, gate_ref, o_ref)


def _conv_weights(w_dw, b_dw):
    ncb = D_MODEL // CONV_COLS
    wd = jnp.pad(w_dw, ((0, CONV_HALO - CONV_WIDTH), (0, 0)))
    wd = wd.reshape(CONV_HALO, ncb, CONV_COLS).transpose(1, 0, 2)
    return wd, b_dw.reshape(ncb, 1, CONV_COLS)


def _conv_pw2_prompt(rows, glu, state, x, mods, w_dw, b_dw, ln_g, ln_b, w, b):
    ncb = D_MODEL // CONV_COLS
    nj = D_MODEL // TN_PROJ
    wd, bd = _conv_weights(w_dw, b_dw)
    halo_per_tile = rows.tm // CONV_HALO
    tps = rows.tiles_per_seq
    const2 = lambda i, j: (0, 0)
    const3 = lambda i, j: (0, 0, 0)
    return pl.pallas_call(
        functools.partial(_conv_prompt_kernel, tiles_per_seq=tps),
        out_shape=jax.ShapeDtypeStruct((rows.m, D_MODEL), F32),
        grid=(rows.m // rows.tm, nj),
        in_specs=[
            pl.BlockSpec((rows.tm, D_MODEL), lambda i, j: (i, 0)),
            pl.BlockSpec((CONV_HALO, D_MODEL), lambda i, j: (jnp.maximum(i * halo_per_tile - 1, 0), 0)),
            pl.BlockSpec((None, CONV_HALO, D_MODEL), lambda i, j: (i // tps, 0, 0)),
            pl.BlockSpec((ncb, CONV_HALO, CONV_COLS), const3),
            pl.BlockSpec((ncb, 1, CONV_COLS), const3),
            pl.BlockSpec((1, D_MODEL), const2),
            pl.BlockSpec((1, D_MODEL), const2),
            pl.BlockSpec((D_MODEL, TN_PROJ), lambda i, j: (0, j)),
            pl.BlockSpec((1, TN_PROJ), lambda i, j: (0, j)),
            pl.BlockSpec((rows.tm, TN_PROJ), lambda i, j: (i, j)),
            _mod_spec(rows, mods, 2, TN_PROJ, lambda j: j),
        ],
        out_specs=pl.BlockSpec((rows.tm, TN_PROJ), lambda i, j: (i, j)),
        scratch_shapes=[
            pltpu.VMEM((ncb, CONV_HALO + rows.tm, CONV_COLS), F32),
            pltpu.VMEM((ncb, rows.tm, CONV_COLS), F32),
            pltpu.VMEM((rows.tm, D_MODEL), BF16),
        ],
        compiler_params=_cparams("parallel", "arbitrary"),
        name="conv_pw2_prompt",
    )(glu, glu, state, wd, bd, ln_g.reshape(1, D_MODEL), ln_b.reshape(1, D_MODEL),
      w, b.reshape(1, D_MODEL), x, mods)


def _conv_pw2_sample(rows, full, x, mods, w_dw, b_dw, ln_g, ln_b, w, b):
    ncb = D_MODEL // CONV_COLS
    nj = D_MODEL // TN_PROJ
    wd, bd = _conv_weights(w_dw, b_dw)
    const2 = lambda i, j: (0, 0)
    const3 = lambda i, j: (0, 0, 0)
    return pl.pallas_call(
        _conv_sample_kernel,
        out_shape=jax.ShapeDtypeStruct((rows.m, D_MODEL), F32),
        grid=(rows.m // rows.tm, nj),
        in_specs=[
            pl.BlockSpec((CONV_WIDTH, rows.tm, D_MODEL), lambda i, j: (0, i, 0)),
            pl.BlockSpec((ncb, CONV_HALO, CONV_COLS), const3),
            pl.BlockSpec((ncb, 1, CONV_COLS), const3),
            pl.BlockSpec((1, D_MODEL), const2),
            pl.BlockSpec((1, D_MODEL), const2),
            pl.BlockSpec((D_MODEL, TN_PROJ), lambda i, j: (0, j)),
            pl.BlockSpec((1, TN_PROJ), lambda i, j: (0, j)),
            pl.BlockSpec((rows.tm, TN_PROJ), lambda i, j: (i, j)),
            _mod_spec(rows, mods, 2, TN_PROJ, lambda j: j),
        ],
        out_specs=pl.BlockSpec((rows.tm, TN_PROJ), lambda i, j: (i, j)),
        scratch_shapes=[
            pltpu.VMEM((ncb, rows.tm, CONV_COLS), F32),
            pltpu.VMEM((rows.tm, D_MODEL), BF16),
        ],
        compiler_params=_cparams("parallel", "arbitrary"),
        name="conv_pw2_sample",
    )(full, wd, bd, ln_g.reshape(1, D_MODEL), ln_b.reshape(1, D_MODEL),
      w, b.reshape(1, D_MODEL), x, mods)


def _mlp_kernel(x_ref, g_ref, sc_ref, sh_ref, gate_ref, w1_ref, w2_ref, o_ref, h_ref):
    f = pl.program_id(1)

    @pl.when(f == 0)
    def _():
        _norm_mod_to_bf16(x_ref, g_ref, sc_ref, sh_ref, h_ref)

    hid = _dot(h_ref[...], w1_ref[...].astype(BF16))
    hid = jnp.square(jnp.maximum(hid, 0.0)).astype(BF16)
    part = _dot(hid, w2_ref[...].astype(BF16))

    @pl.when(f == 0)
    def _():
        o_ref[...] = part

    @pl.when(f > 0)
    def _():
        o_ref[...] += part

    @pl.when(f == pl.num_programs(1) - 1)
    def _():
        o_ref[...] = x_ref[...] + gate_ref[...] * o_ref[...]


def _mlp(rows, x, mods, norm_g, w1, w2):
    d_ff = w1.shape[1]
    full = lambda j: 0
    return pl.pallas_call(
        _mlp_kernel,
        out_shape=jax.ShapeDtypeStruct((rows.m, D_MODEL), F32),
        grid=(rows.m // rows.tm, d_ff // TF_MLP),
        in_specs=[
            pl.BlockSpec((rows.tm, D_MODEL), lambda i, f: (i, 0)),
            pl.BlockSpec((1, D_MODEL), lambda i, f: (0, 0)),
            _mod_spec(rows, mods, 4, D_MODEL, full),
            _mod_spec(rows, mods, 3, D_MODEL, full),
            _mod_spec(rows, mods, 5, D_MODEL, full),
            pl.BlockSpec((D_MODEL, TF_MLP), lambda i, f: (0, f)),
            pl.BlockSpec((TF_MLP, D_MODEL), lambda i, f: (f, 0)),
        ],
        out_specs=pl.BlockSpec((rows.tm, D_MODEL), lambda i, f: (i, 0)),
        scratch_shapes=[pltpu.VMEM((rows.tm, D_MODEL), BF16)],
        compiler_params=_cparams("parallel", "arbitrary"),
        name="sqrelu_mlp",
    )(x, norm_g.reshape(1, D_MODEL), mods, mods, mods, w1, w2)


def _proj_kernel(*refs, qk_norm, n_out):
    if qk_norm:
        x_ref, g_ref, sc_ref, sh_ref, w_ref, ng_ref, cos_ref, sa_ref, sb_ref = refs[:9]
        rest = refs[9:]
    else:
        x_ref, g_ref, sc_ref, sh_ref, w_ref = refs[:5]
        rest = refs[5:]
    out_refs, h_ref = rest[:n_out], rest[n_out]

    @pl.when(pl.program_id(1) == 0)
    def _():
        _norm_mod_to_bf16(x_ref, g_ref, sc_ref, sh_ref, h_ref)

    acc = _dot(h_ref[...], w_ref[...].astype(BF16))
    tn = acc.shape[1]
    if not qk_norm:
        for o_ref in out_refs:
            o_ref[...] = acc.astype(o_ref.dtype)
        return

    cos, sa, sb, ng = cos_ref[...], sa_ref[...], sb_ref[...], ng_ref[...]
    for grp in range(tn // HEAD_DIM):
        cols = slice(grp * HEAD_DIM, (grp + 1) * HEAD_DIM)
        a = acc[:, cols]
        y = a * lax.rsqrt(jnp.mean(a * a, axis=-1, keepdims=True) + EPS) * ng
        y = (y * cos + pltpu.roll(y, HEAD_DIM - ROT_DIM // 2, 1) * sa
             + pltpu.roll(y, ROT_DIM // 2, 1) * sb)
        for o_ref in out_refs:
            o_ref[:, cols] = y.astype(o_ref.dtype)


def _proj(rows, x, mods, k_shift, k_scale, norm_g, w, out_dtypes, rope=None, head_g=None):
    n = w.shape[1]
    full = lambda j: 0
    qk_norm = rope is not None
    in_specs = [
        pl.BlockSpec((rows.tm, D_MODEL), lambda i, j: (i, 0)),
        pl.BlockSpec((1, D_MODEL), lambda i, j: (0, 0)),
        _mod_spec(rows, mods, k_scale, D_MODEL, full),
        _mod_spec(rows, mods, k_shift, D_MODEL, full),
        pl.BlockSpec((D_MODEL, TN_PROJ), lambda i, j: (0, j)),
    ]
    args = [x, norm_g.reshape(1, D_MODEL), mods, mods, w]
    if qk_norm:
        tps = rows.tiles_per_seq
        in_specs.append(pl.BlockSpec((1, HEAD_DIM), lambda i, j: (0, 0)))
        args.append(head_g.reshape(1, HEAD_DIM))
        for t in rope:
            in_specs.append(pl.BlockSpec((rows.tm, HEAD_DIM), lambda i, j: (i % tps, 0)))
            args.append(t)
    outs = pl.pallas_call(
        functools.partial(_proj_kernel, qk_norm=qk_norm, n_out=len(out_dtypes)),
        out_shape=[jax.ShapeDtypeStruct((rows.m, n), dt) for dt in out_dtypes],
        grid=(rows.m // rows.tm, n // TN_PROJ),
        in_specs=in_specs,
        out_specs=[pl.BlockSpec((rows.tm, TN_PROJ), lambda i, j: (i, j)) for _ in out_dtypes],
        scratch_shapes=[pltpu.VMEM((rows.tm, D_MODEL), BF16)],
        compiler_params=_cparams("parallel", "arbitrary"),
        name="norm_proj_rope" if qk_norm else "norm_proj",
    )(*args)
    return outs


def _res_linear_kernel(a_ref, w_ref, x_ref, gate_ref, o_ref):
    out = _dot(a_ref[...].astype(BF16), w_ref[...].astype(BF16))
    o_ref[...] = x_ref[...] + gate_ref[...] * out


def _res_linear(rows, a, w, x, mods, k_gate):
    kdim, n = w.shape
    return pl.pallas_call(
        _res_linear_kernel,
        out_shape=jax.ShapeDtypeStruct((rows.m, n), F32),
        grid=(rows.m // rows.tm, n // TN_PROJ),
        in_specs=[
            pl.BlockSpec((rows.tm, kdim), lambda i, j: (i, 0)),
            pl.BlockSpec((kdim, TN_PROJ), lambda i, j: (0, j)),
            pl.BlockSpec((rows.tm, TN_PROJ), lambda i, j: (i, j)),
            _mod_spec(rows, mods, k_gate, TN_PROJ, lambda j: j),
        ],
        out_specs=pl.BlockSpec((rows.tm, TN_PROJ), lambda i, j: (i, j)),
        compiler_params=_cparams("parallel", "arbitrary"),
        name="out_proj_residual",
    )(a, w, x, mods)


def _lambda_full(lq1_ref, lk1_ref, lq2_ref, lk2_ref, lam_init):
    d1 = jnp.sum(lq1_ref[...] * lk1_ref[...], axis=-1, keepdims=True)
    d2 = jnp.sum(lq2_ref[...] * lk2_ref[...], axis=-1, keepdims=True)
    return jnp.exp(d1) - jnp.exp(d2) + lam_init


def _subln(o, g, lam_init):
    y = o * lax.rsqrt(jnp.mean(o * o, axis=-1, keepdims=True) + EPS)
    return y * g * (1.0 - lam_init)


def _attn_prompt_kernel(q_ref, k_ref, v_ref, lq1_ref, lk1_ref, lq2_ref, lk2_ref, sg_ref,
                        o_ref, m_ref, l_ref, acc_ref, *, lam_init):
    tq = q_ref.shape[0]
    i = pl.program_id(2)
    m_ref[...] = jnp.full(m_ref.shape, NEG_BIG, F32)
    l_ref[...] = jnp.zeros(l_ref.shape, F32)
    acc_ref[...] = jnp.zeros(acc_ref.shape, F32)

    def chunk(c, masked):
        k0 = pl.multiple_of(c * tq, tq)
        vc = v_ref[pl.ds(k0, tq), :]
        for comp in range(2):
            cols = slice(comp * HEAD_DIM, (comp + 1) * HEAD_DIM)
            s = lax.dot_general(q_ref[:, cols], k_ref[pl.ds(k0, tq), cols],
                                (((1,), (1,)), ((), ())), preferred_element_type=F32) * SCALE
            if masked:
                row = lax.broadcasted_iota(jnp.int32, s.shape, 0)
                col = lax.broadcasted_iota(jnp.int32, s.shape, 1)
                s = jnp.where(col <= row, s, NEG_BIG)
            m_old = m_ref[comp]
            m_new = jnp.maximum(m_old, jnp.max(s, axis=-1, keepdims=True))
            p = jnp.exp(s - m_new)
            alpha = jnp.exp(m_old - m_new)
            l_ref[comp] = alpha * l_ref[comp] + jnp.sum(p, axis=-1, keepdims=True)
            acc_ref[comp] = alpha * acc_ref[comp] + _dot(p.astype(BF16), vc)
            m_ref[comp] = m_new

    def full_chunk(c, carry):
        chunk(c, False)
        return carry

    lax.fori_loop(0, i, full_chunk, 0)
    chunk(i, True)

    lam = _lambda_full(lq1_ref, lk1_ref, lq2_ref, lk2_ref, lam_init)
    o = acc_ref[0] / l_ref[0] - lam * (acc_ref[1] / l_ref[1])
    o_ref[...] = _subln(o, sg_ref[...], lam_init).astype(o_ref.dtype)


def _attn_prompt(q, k, v, lam_params, subln_g, lam_init, n_seq, seq):
    nq = seq // TQ_ATTN
    lam_spec = pl.BlockSpec((1, HEAD_DIM), lambda b, h, i: (0, 0))
    return pl.pallas_call(
        functools.partial(_attn_prompt_kernel, lam_init=lam_init),
        out_shape=jax.ShapeDtypeStruct(q.shape, BF16),
        grid=(n_seq, N_HEADS, nq),
        in_specs=[
            pl.BlockSpec((TQ_ATTN, V_DIM), lambda b, h, i: (b * nq + i, h)),
            pl.BlockSpec((seq, V_DIM), lambda b, h, i: (b, h)),
            pl.BlockSpec((seq, V_DIM), lambda b, h, i: (b, h)),
            lam_spec, lam_spec, lam_spec, lam_spec,
            pl.BlockSpec((1, V_DIM), lambda b, h, i: (0, 0)),
        ],
        out_specs=pl.BlockSpec((TQ_ATTN, V_DIM), lambda b, h, i: (b * nq + i, h)),
        scratch_shapes=[
            pltpu.VMEM((2, TQ_ATTN, 1), F32),
            pltpu.VMEM((2, TQ_ATTN, 1), F32),
            pltpu.VMEM((2, TQ_ATTN, V_DIM), F32),
        ],
        compiler_params=_cparams("parallel", "parallel", "arbitrary"),
        name="diff_attn_prompt",
    )(q, k, v, *[p.reshape(1, HEAD_DIM) for p in lam_params], subln_g.reshape(1, V_DIM))


def _attn_sample_kernel(pt_ref, q_ref, kn_ref, vn_ref, lq1_ref, lk1_ref, lq2_ref, lk2_ref, sg_ref,
                        *refs, lam_init):
    npg = PAGES_PER_STEP
    k_refs, v_refs = refs[:npg], refs[npg:2 * npg]
    o_ref, m_ref, l_ref, acc_ref = refs[2 * npg:]
    c = pl.program_id(1)
    n_rows = 2 * N_HEADS
    width = N_HEADS * V_DIM

    @pl.when(c == 0)
    def _():
        m_ref[...] = jnp.full(m_ref.shape, NEG_BIG, F32)
        l_ref[...] = jnp.zeros(l_ref.shape, F32)
        acc_ref[...] = jnp.zeros(acc_ref.shape, F32)

    row = lax.broadcasted_iota(jnp.int32, (n_rows, width), 0)
    lane = lax.broadcasted_iota(jnp.int32, (n_rows, width), 1)
    q_bd = jnp.where(jnp.right_shift(lane, HEAD_DIM.bit_length() - 1) == row, q_ref[...], 0.0)
    q_bf = q_bd.astype(BF16)

    s = jnp.concatenate(
        [lax.dot_general(q_bf, k_refs[p][...].astype(BF16), (((1,), (1,)), ((), ())),
                         preferred_element_type=F32) for p in range(npg)], axis=1) * SCALE
    m_old = m_ref[...]
    m_new = jnp.maximum(m_old, jnp.max(s, axis=-1, keepdims=True))
    p_all = jnp.exp(s - m_new)
    alpha = jnp.exp(m_old - m_new)
    l_ref[...] = alpha * l_ref[...] + jnp.sum(p_all, axis=-1, keepdims=True)
    p_bf = p_all.astype(BF16)
    pv = _dot(p_bf[:, 0:PAGE_SIZE], v_refs[0][...].astype(BF16))
    for p in range(1, npg):
        pv = pv + _dot(p_bf[:, p * PAGE_SIZE:(p + 1) * PAGE_SIZE], v_refs[p][...].astype(BF16))
    acc_ref[...] = alpha * acc_ref[...] + pv
    m_ref[...] = m_new

    @pl.when(c == pl.num_programs(1) - 1)
    def _():
        s_new = jnp.sum(q_bd * kn_ref[...], axis=-1, keepdims=True) * SCALE
        m_fin = jnp.maximum(m_ref[...], s_new)
        a = jnp.exp(m_ref[...] - m_fin)
        p_new = jnp.exp(s_new - m_fin)
        l_fin = a * l_ref[...] + p_new
        w_all = (a * acc_ref[...] + p_new * vn_ref[...]) / l_fin
        head = jnp.right_shift(lane, V_DIM.bit_length() - 1)
        o1 = jnp.sum(jnp.where(row == 2 * head, w_all, 0.0), axis=0, keepdims=True)
        o2 = jnp.sum(jnp.where(row == 2 * head + 1, w_all, 0.0), axis=0, keepdims=True)
        lam = _lambda_full(lq1_ref, lk1_ref, lq2_ref, lk2_ref, lam_init)
        o = o1 - lam * o2
        for h in range(N_HEADS):
            cols = slice(h * V_DIM, (h + 1) * V_DIM)
            o_ref[:, cols] = _subln(o[:, cols], sg_ref[...], lam_init)


def _attn_sample(q, k_new, v_new, cache_k, cache_v, page_table, lam_params, subln_g, lam_init):
    n_seq, n_pages = page_table.shape
    width = N_HEADS * V_DIM
    npg = PAGES_PER_STEP
    n_chunks = n_pages // npg
    row_spec = pl.BlockSpec((None, 1, width), lambda b, c, pt: (b, 0, 0))
    lam_spec = pl.BlockSpec((1, HEAD_DIM), lambda b, c, pt: (0, 0))

    def page_spec(p):
        return pl.BlockSpec((None, PAGE_SIZE, width),
                            lambda b, c, pt: (pt[b * n_pages + c * npg + p], 0, 0))

    grid_spec = pltpu.PrefetchScalarGridSpec(
        num_scalar_prefetch=1,
        grid=(n_seq, n_chunks),
        in_specs=[row_spec, row_spec, row_spec, lam_spec, lam_spec, lam_spec, lam_spec,
                  pl.BlockSpec((1, V_DIM), lambda b, c, pt: (0, 0))]
                 + [page_spec(p) for p in range(npg)] + [page_spec(p) for p in range(npg)],
        out_specs=row_spec,
        scratch_shapes=[
            pltpu.VMEM((2 * N_HEADS, 1), F32),
            pltpu.VMEM((2 * N_HEADS, 1), F32),
            pltpu.VMEM((2 * N_HEADS, width), F32),
        ],
    )
    return pl.pallas_call(
        functools.partial(_attn_sample_kernel, lam_init=lam_init),
        out_shape=jax.ShapeDtypeStruct((n_seq, 1, width), F32),
        grid_spec=grid_spec,
        compiler_params=_cparams("parallel", "arbitrary"),
        name="diff_attn_sample",
    )(page_table.reshape(-1), q, k_new, v_new,
      *[p.reshape(1, HEAD_DIM) for p in lam_params], subln_g.reshape(1, V_DIM),
      *([cache_k] * npg), *([cache_v] * npg))


def _rope_tables(pos):
    half = ROT_DIM // 2
    inv_freq = jnp.power(ROPE_THETA, -jnp.arange(half, dtype=F32) * 2.0 / ROT_DIM)
    ang = pos.astype(F32)[:, None] * inv_freq[None, :]
    cos, sin = jnp.cos(ang), jnp.sin(ang)
    n = pos.shape[0]
    ones = jnp.ones((n, HEAD_DIM - ROT_DIM), F32)
    zeros_tail = jnp.zeros((n, HEAD_DIM - half), F32)
    cos_t = jnp.concatenate([cos, cos, ones], axis=1)
    sin_a = jnp.concatenate([-sin, zeros_tail], axis=1)
    sin_b = jnp.concatenate([jnp.zeros((n, half), F32), sin, jnp.zeros((n, HEAD_DIM - ROT_DIM), F32)], axis=1)
    return cos_t, sin_a, sin_b


def _lam_init(layer):
    return 0.8 - 0.6 * math.exp(-0.3 * layer)


def kernel(x_prompt, x_sample, state_conv, cache_k, cache_v, page_table, c_prompt, c_sample,
           ada_w, ada_b, norm_mix_g, norm_ff_g, w_ff1, w_ff2,
           conv_w_pw1, conv_b_pw1, conv_w_dw, conv_b_dw, conv_ln_g, conv_ln_b, conv_w_pw2, conv_b_pw2,
           kv_ada_w, kv_ada_b, kv_norm_g, w_k, w_v, k_norm_g,
           w_q, q_norm_g, lambda_q1, lambda_k1, lambda_q2, lambda_k2, subln_g, w_o):
    n_seq, seq, _ = x_prompt.shape
    n_dec = x_sample.shape[0]
    assert x_sample.shape[1] == 1 and n_dec <= SAMPLE_ROWS
    assert seq % TM_PROMPT == 0 and seq % TQ_ATTN == 0
    n_pages = page_table.shape[1]
    assert n_pages % PAGES_PER_STEP == 0
    n_past = n_pages * PAGE_SIZE
    width = N_HEADS * V_DIM

    rows_p = Rows(m=n_seq * seq, tm=TM_PROMPT, tiles_per_seq=seq // TM_PROMPT, per_row=False)
    rows_s = Rows(m=SAMPLE_ROWS, tm=SAMPLE_ROWS, tiles_per_seq=1, per_row=True)
    pad_s = SAMPLE_ROWS - n_dec

    c_all = jnp.concatenate([c_sample, c_prompt,
                             jnp.zeros((SAMPLE_ROWS - n_dec - n_seq, D_MODEL), F32)], axis=0)
    mods_all = _ada(c_all, ada_w, ada_b)
    kv_mods_all = _ada(c_all, kv_ada_w[None], kv_ada_b[None])[0]

    def group_mods(m, rows):
        if rows.per_row:
            return m
        return m[n_dec:n_dec + n_seq].reshape(n_seq, 1, m.shape[-1])

    cache_k2 = cache_k.reshape(cache_k.shape[0], PAGE_SIZE, width)
    cache_v2 = cache_v.reshape(cache_v.shape[0], PAGE_SIZE, width)

    def trunk(rows, x, rope, conv_fn, attn_fn):
        new_glu = []
        kv_mods = group_mods(kv_mods_all, rows)
        k_f32 = v_f32 = k_att = v_att = None
        for l in range(DEPTH):
            mods = group_mods(mods_all[l], rows)
            if l == N_A_LAYERS:
                k_f32, k_bf = _proj(rows, x, kv_mods, 0, 1, kv_norm_g, w_k, (F32, BF16),
                                    rope=rope, head_g=k_norm_g)
                v_f32, v_bf = _proj(rows, x, kv_mods, 0, 1, kv_norm_g, w_v, (F32, BF16))
                k_att, v_att = (k_f32, v_f32) if rows.per_row else (k_bf, v_bf)
            if l < N_A_LAYERS:
                glu = _pw1_glu(rows, x, mods, norm_mix_g[l], conv_w_pw1[l], conv_b_pw1[l])
                new_glu.append(glu)
                x = conv_fn(l, glu, x, mods)
            else:
                j = l - N_A_LAYERS
                q_dtype = F32 if rows.per_row else BF16
                (q,) = _proj(rows, x, mods, 0, 1, norm_mix_g[l], w_q[j], (q_dtype,),
                             rope=rope, head_g=q_norm_g[j])
                lam_params = (lambda_q1[j], lambda_k1[j], lambda_q2[j], lambda_k2[j])
                o = attn_fn(q, k_att, v_att, lam_params, subln_g[j], _lam_init(l))
                x = _res_linear(rows, o, w_o[j], x, mods, 2)
            x = _mlp(rows, x, mods, norm_ff_g[l], w_ff1[l], w_ff2[l])
        return x, new_glu, k_f32, v_f32

    rope_p = _rope_tables(jnp.arange(seq))
    state_p = jnp.zeros((n_seq, CONV_HALO, D_MODEL), F32)

    def conv_prompt(l, glu, x, mods):
        return _conv_pw2_prompt(rows_p, glu, state_p, x, mods, conv_w_dw[l], conv_b_dw[l],
                                conv_ln_g[l], conv_ln_b[l], conv_w_pw2[l], conv_b_pw2[l])

    def attn_prompt(q, k, v, lam_params, sg, lam_init):
        return _attn_prompt(q, k, v, lam_params, sg, lam_init, n_seq, seq)

    y_p, glu_p, k_p, v_p = trunk(rows_p, x_prompt.reshape(n_seq * seq, D_MODEL), rope_p,
                                 conv_prompt, attn_prompt)
    conv_p = jnp.stack([g.reshape(n_seq, seq, D_MODEL)[:, seq - (CONV_WIDTH - 1):] for g in glu_p])

    rope_s = _rope_tables(jnp.full((SAMPLE_ROWS,), n_past, jnp.int32))
    conv_s_rows = []

    def conv_sample(l, glu, x, mods):
        hist = jnp.pad(state_conv[l].transpose(1, 0, 2), ((0, 0), (0, pad_s), (0, 0)))
        full = jnp.concatenate([hist, glu[None]], axis=0)
        conv_s_rows.append(full[1:, :n_dec].transpose(1, 0, 2))
        return _conv_pw2_sample(rows_s, full, x, mods, conv_w_dw[l], conv_b_dw[l],
                                conv_ln_g[l], conv_ln_b[l], conv_w_pw2[l], conv_b_pw2[l])

    def attn_sample(q, k, v, lam_params, sg, lam_init):
        o = _attn_sample(q[:n_dec].reshape(n_dec, 1, width), k[:n_dec].reshape(n_dec, 1, width),
                         v[:n_dec].reshape(n_dec, 1, width), cache_k2, cache_v2, page_table,
                         lam_params, sg, lam_init)
        return jnp.pad(o.reshape(n_dec, width), ((0, pad_s), (0, 0)))

    x_s = jnp.pad(x_sample.reshape(n_dec, D_MODEL), ((0, pad_s), (0, 0)))
    y_s, _, k_s, v_s = trunk(rows_s, x_s, rope_s, conv_sample, attn_sample)

    return (y_p.reshape(n_seq, seq, D_MODEL),
            y_s[:n_dec].reshape(n_dec, 1, D_MODEL),
            conv_p,
            jnp.stack(conv_s_rows),
            k_p.reshape(n_seq, seq, N_HEADS, 2, HEAD_DIM),
            v_p.reshape(n_seq, seq, N_HEADS, V_DIM),
            k_s[:n_dec].reshape(n_dec, 1, N_HEADS, 2, HEAD_DIM),
            v_s[:n_dec].reshape(n_dec, 1, N_HEADS, V_DIM))
```

```python
import functools
import math
from typing import NamedTuple

import jax
import jax.numpy as jnp
from jax import lax
from jax.experimental import pallas as pl
from jax.experimental.pallas import tpu as pltpu

F32 = jnp.float32
BF16 = jnp.bfloat16

D_MODEL = 2048
DEPTH = 4
N_A_LAYERS = DEPTH // 2
CONV_WIDTH = 31
HEAD_DIM = 128
V_DIM = 2 * HEAD_DIM
N_HEADS = D_MODEL // V_DIM
ROT_DIM = HEAD_DIM // 4
ROPE_THETA = 500000.0
EPS = 1e-6
SCALE = HEAD_DIM ** -0.5
PAGE_SIZE = 128
NEG_BIG = -1e30

LANE = 128
SUBLANE = 8
VMEM_LIMIT_BYTES = 56 * 1024 * 1024

SAMPLE_ROWS = 16
TM_PROMPT = 512
TN_PROJ = 1024
TN_GLU = 512
TF_MLP = 512
TN_ADA = 1024
NORM_ROWS = 64
CONV_COLS = 256
CONV_ROWS = 64
CONV_HALO = 32
TQ_ATTN = 512
PAGES_PER_STEP = 8


class Rows(NamedTuple):
    m: int
    tm: int
    tiles_per_seq: int
    per_row: bool


def _cparams(*sem):
    return pltpu.CompilerParams(dimension_semantics=sem, vmem_limit_bytes=VMEM_LIMIT_BYTES)


def _mod_spec(rows, mods, k, width, col_of_j):
    nb = D_MODEL // width
    if rows.per_row:
        return pl.BlockSpec((rows.tm, width), lambda i, j: (0, k * nb + col_of_j(j)))
    return pl.BlockSpec((None, 1, width), lambda i, j: (i // rows.tiles_per_seq, 0, k * nb + col_of_j(j)))


def _rows_of(ref, r0, n):
    if ref.shape[0] == 1:
        return ref[...]
    return ref[pl.ds(r0, n), :]


def _norm_mod_to_bf16(x_ref, g_ref, sc_ref, sh_ref, h_ref):
    tm = x_ref.shape[0]
    rb = min(tm, NORM_ROWS)

    def body(r, carry):
        r0 = pl.multiple_of(r * rb, rb)
        x = x_ref[pl.ds(r0, rb), :]
        y = x * lax.rsqrt(jnp.mean(x * x, axis=-1, keepdims=True) + EPS)
        y = y * g_ref[...]
        h = y * (1.0 + _rows_of(sc_ref, r0, rb)) + _rows_of(sh_ref, r0, rb)
        h_ref[pl.ds(r0, rb), :] = h.astype(BF16)
        return carry

    lax.fori_loop(0, tm // rb, body, 0)


def _dot(a, b):
    return jnp.dot(a, b, preferred_element_type=F32)


def _ada_kernel(c_ref, w_ref, b_ref, o_ref):
    c = c_ref[...]
    s = (c * jax.nn.sigmoid(c)).astype(BF16)
    o_ref[...] = _dot(s, w_ref[...].astype(BF16)) + b_ref[...]


def _ada(c_all, w, b):
    n_layers, _, n = w.shape
    r = c_all.shape[0]
    return pl.pallas_call(
        _ada_kernel,
        out_shape=jax.ShapeDtypeStruct((n_layers, r, n), F32),
        grid=(n_layers, n // TN_ADA),
        in_specs=[
            pl.BlockSpec((r, D_MODEL), lambda l, j: (0, 0)),
            pl.BlockSpec((None, D_MODEL, TN_ADA), lambda l, j: (l, 0, j)),
            pl.BlockSpec((None, 1, TN_ADA), lambda l, j: (l, 0, j)),
        ],
        out_specs=pl.BlockSpec((None, r, TN_ADA), lambda l, j: (l, 0, j)),
        compiler_params=_cparams("parallel", "parallel"),
        name="ada_mod",
    )(c_all, w, b.reshape(n_layers, 1, n))


def _pw1_glu_kernel(x_ref, g_ref, sc_ref, sh_ref, wa_ref, wg_ref, ba_ref, bg_ref, o_ref, h_ref):
    @pl.when(pl.program_id(1) == 0)
    def _():
        _norm_mod_to_bf16(x_ref, g_ref, sc_ref, sh_ref, h_ref)

    h = h_ref[...]
    a = _dot(h, wa_ref[...].astype(BF16)) + ba_ref[...]
    gt = _dot(h, wg_ref[...].astype(BF16)) + bg_ref[...]
    o_ref[...] = a * jax.nn.sigmoid(gt)


def _pw1_glu(rows, x, mods, norm_g, w, layer, b):
    nj = D_MODEL // TN_GLU
    b2 = b.reshape(1, 2 * D_MODEL)
    full = lambda j: 0
    return pl.pallas_call(
        _pw1_glu_kernel,
        out_shape=jax.ShapeDtypeStruct((rows.m, D_MODEL), F32),
        grid=(rows.m // rows.tm, nj),
        in_specs=[
            pl.BlockSpec((rows.tm, D_MODEL), lambda i, j: (i, 0)),
            pl.BlockSpec((1, D_MODEL), lambda i, j: (0, 0)),
            _mod_spec(rows, mods, 1, D_MODEL, full),
            _mod_spec(rows, mods, 0, D_MODEL, full),
            pl.BlockSpec((None, D_MODEL, TN_GLU), lambda i, j: (layer, 0, j)),
            pl.BlockSpec((None, D_MODEL, TN_GLU), lambda i, j: (layer, 0, j + nj)),
            pl.BlockSpec((1, TN_GLU), lambda i, j: (0, j)),
            pl.BlockSpec((1, TN_GLU), lambda i, j: (0, j + nj)),
        ],
        out_specs=pl.BlockSpec((rows.tm, TN_GLU), lambda i, j: (i, j)),
        scratch_shapes=[pltpu.VMEM((rows.tm, D_MODEL), BF16)],
        compiler_params=_cparams("parallel", "arbitrary"),
        name="pw1_glu",
    )(x, norm_g.reshape(1, D_MODEL), mods, mods, w, w, b2, b2)


def _ln_silu_to_bf16(yc_ref, lng_ref, lnb_ref, y_ref):
    ncb, tm, cw = yc_ref.shape
    rb = min(tm, NORM_ROWS)

    def body(r, carry):
        r0 = pl.multiple_of(r * rb, rb)
        parts = [yc_ref[cb, pl.ds(r0, rb), :] for cb in range(ncb)]
        mu = sum(jnp.sum(p, axis=-1, keepdims=True) for p in parts) * (1.0 / D_MODEL)
        var = sum(jnp.sum(jnp.square(p - mu), axis=-1, keepdims=True) for p in parts) * (1.0 / D_MODEL)
        inv = lax.rsqrt(var + EPS)
        for cb in range(ncb):
            cols = slice(cb * cw, (cb + 1) * cw)
            y = (parts[cb] - mu) * inv * lng_ref[:, cols] + lnb_ref[:, cols]
            y = y * jax.nn.sigmoid(y)
            y_ref[pl.ds(r0, rb), cols] = y.astype(BF16)
        return carry

    lax.fori_loop(0, tm // rb, body, 0)


def _pw2_residual(y_ref, w_ref, b_ref, x_ref, gate_ref, o_ref):
    out = _dot(y_ref[...], w_ref[...].astype(BF16)) + b_ref[...]
    o_ref[...] = x_ref[...] + gate_ref[...] * out


def _conv_prompt_kernel(glu_ref, halo_ref, st_ref, wd_ref, bd_ref, lng_ref, lnb_ref,
                        w_ref, b_ref, x_ref, gate_ref, o_ref, win_ref, yc_ref, y_ref, *, tiles_per_seq):
    ncb, _, cw = win_ref.shape
    tm = glu_ref.shape[0]

    @pl.when(pl.program_id(1) == 0)
    def _():
        first = pl.program_id(0) % tiles_per_seq == 0
        for cb in range(ncb):
            cols = slice(cb * cw, (cb + 1) * cw)
            win_ref[cb, 0:CONV_HALO, :] = jnp.where(first, st_ref[:, cols], halo_ref[:, cols])
            win_ref[cb, CONV_HALO:, :] = glu_ref[:, cols]

        base = CONV_HALO - (CONV_WIDTH - 1)

        def conv_cols(cb, carry):
            wd = wd_ref[cb]
            for rb in range(tm // CONV_ROWS):
                acc = jnp.zeros((CONV_ROWS, cw), F32)
                for w in range(CONV_WIDTH):
                    acc = acc + win_ref[cb, pl.ds(rb * CONV_ROWS + base + w, CONV_ROWS), :] * wd[w:w + 1, :]
                yc_ref[cb, pl.ds(rb * CONV_ROWS, CONV_ROWS), :] = acc + bd_ref[cb]
            return carry

        lax.fori_loop(0, ncb, conv_cols, 0)
        _ln_silu_to_bf16(yc_ref, lng_ref, lnb_ref, y_ref)

    _pw2_residual(y_ref, w_ref, b_ref, x_ref, gate_ref, o_ref)


def _conv_sample_kernel(full_ref, wd_ref, bd_ref, lng_ref, lnb_ref,
                        w_ref, b_ref, x_ref, gate_ref, o_ref, yc_ref, y_ref):
    ncb, _, cw = yc_ref.shape

    @pl.when(pl.program_id(1) == 0)
    def _():
        for cb in range(ncb):
            cols = slice(cb * cw, (cb + 1) * cw)
            wd = wd_ref[cb]
            acc = jnp.zeros((full_ref.shape[1], cw), F32)
            for w in range(CONV_WIDTH):
                acc = acc + full_ref[w, :, cols] * wd[w:w + 1, :]
            yc_ref[cb] = acc + bd_ref[cb]
        _ln_silu_to_bf16(yc_ref, lng_ref, lnb_ref, y_ref)

    _pw2_residual(y_ref, w_ref, b_ref, x_ref, gate_ref, o_ref)


def _conv_weights(w_dw, b_dw):
    ncb = D_MODEL // CONV_COLS
    wd = jnp.pad(w_dw, ((0, CONV_HALO - CONV_WIDTH), (0, 0)))
    wd = wd.reshape(CONV_HALO, ncb, CONV_COLS).transpose(1, 0, 2)
    return wd, b_dw.reshape(ncb, 1, CONV_COLS)


def _conv_pw2_prompt(rows, glu, state, x, mods, w_dw, b_dw, ln_g, ln_b, w, layer, b):
    ncb = D_MODEL // CONV_COLS
    nj = D_MODEL // TN_PROJ
    wd, bd = _conv_weights(w_dw, b_dw)
    halo_per_tile = rows.tm // CONV_HALO
    tps = rows.tiles_per_seq
    const2 = lambda i, j: (0, 0)
    const3 = lambda i, j: (0, 0, 0)
    return pl.pallas_call(
        functools.partial(_conv_prompt_kernel, tiles_per_seq=tps),
        out_shape=jax.ShapeDtypeStruct((rows.m, D_MODEL), F32),
        grid=(rows.m // rows.tm, nj),
        in_specs=[
            pl.BlockSpec((rows.tm, D_MODEL), lambda i, j: (i, 0)),
            pl.BlockSpec((CONV_HALO, D_MODEL), lambda i, j: (jnp.maximum(i * halo_per_tile - 1, 0), 0)),
            pl.BlockSpec((None, CONV_HALO, D_MODEL), lambda i, j: (i // tps, 0, 0)),
            pl.BlockSpec((ncb, CONV_HALO, CONV_COLS), const3),
            pl.BlockSpec((ncb, 1, CONV_COLS), const3),
            pl.BlockSpec((1, D_MODEL), const2),
            pl.BlockSpec((1, D_MODEL), const2),
            pl.BlockSpec((None, D_MODEL, TN_PROJ), lambda i, j: (layer, 0, j)),
            pl.BlockSpec((1, TN_PROJ), lambda i, j: (0, j)),
            pl.BlockSpec((rows.tm, TN_PROJ), lambda i, j: (i, j)),
            _mod_spec(rows, mods, 2, TN_PROJ, lambda j: j),
        ],
        out_specs=pl.BlockSpec((rows.tm, TN_PROJ), lambda i, j: (i, j)),
        scratch_shapes=[
            pltpu.VMEM((ncb, CONV_HALO + rows.tm, CONV_COLS), F32),
            pltpu.VMEM((ncb, rows.tm, CONV_COLS), F32),
            pltpu.VMEM((rows.tm, D_MODEL), BF16),
        ],
        compiler_params=_cparams("parallel", "arbitrary"),
        name="conv_pw2_prompt",
    )(glu, glu, state, wd, bd, ln_g.reshape(1, D_MODEL), ln_b.reshape(1, D_MODEL),
      w, b.reshape(1, D_MODEL), x, mods)


def _conv_pw2_sample(rows, full, x, mods, w_dw, b_dw, ln_g, ln_b, w, layer, b):
    ncb = D_MODEL // CONV_COLS
    nj = D_MODEL // TN_PROJ
    wd, bd = _conv_weights(w_dw, b_dw)
    const2 = lambda i, j: (0, 0)
    const3 = lambda i, j: (0, 0, 0)
    return pl.pallas_call(
        _conv_sample_kernel,
        out_shape=jax.ShapeDtypeStruct((rows.m, D_MODEL), F32),
        grid=(rows.m // rows.tm, nj),
        in_specs=[
            pl.BlockSpec((CONV_WIDTH, rows.tm, D_MODEL), lambda i, j: (0, i, 0)),
            pl.BlockSpec((ncb, CONV_HALO, CONV_COLS), const3),
            pl.BlockSpec((ncb, 1, CONV_COLS), const3),
            pl.BlockSpec((1, D_MODEL), const2),
            pl.BlockSpec((1, D_MODEL), const2),
            pl.BlockSpec((None, D_MODEL, TN_PROJ), lambda i, j: (layer, 0, j)),
            pl.BlockSpec((1, TN_PROJ), lambda i, j: (0, j)),
            pl.BlockSpec((rows.tm, TN_PROJ), lambda i, j: (i, j)),
            _mod_spec(rows, mods, 2, TN_PROJ, lambda j: j),
        ],
        out_specs=pl.BlockSpec((rows.tm, TN_PROJ), lambda i, j: (i, j)),
        scratch_shapes=[
            pltpu.VMEM((ncb, rows.tm, CONV_COLS), F32),
            pltpu.VMEM((rows.tm, D_MODEL), BF16),
        ],
        compiler_params=_cparams("parallel", "arbitrary"),
        name="conv_pw2_sample",
    )(full, wd, bd, ln_g.reshape(1, D_MODEL), ln_b.reshape(1, D_MODEL),
      w, b.reshape(1, D_MODEL), x, mods)


def _mlp_kernel(x_ref, g_ref, sc_ref, sh_ref, gate_ref, w1_ref, w2_ref, o_ref, h_ref):
    f = pl.program_id(1)

    @pl.when(f == 0)
    def _():
        _norm_mod_to_bf16(x_ref, g_ref, sc_ref, sh_ref, h_ref)

    hid = _dot(h_ref[...], w1_ref[...].astype(BF16))
    hid = jnp.square(jnp.maximum(hid, 0.0)).astype(BF16)
    part = _dot(hid, w2_ref[...].astype(BF16))

    @pl.when(f == 0)
    def _():
        o_ref[...] = part

    @pl.when(f > 0)
    def _():
        o_ref[...] += part

    @pl.when(f == pl.num_programs(1) - 1)
    def _():
        o_ref[...] = x_ref[...] + gate_ref[...] * o_ref[...]


def _mlp(rows, x, mods, norm_g, w1, w2, layer):
    d_ff = w1.shape[2]
    full = lambda j: 0
    return pl.pallas_call(
        _mlp_kernel,
        out_shape=jax.ShapeDtypeStruct((rows.m, D_MODEL), F32),
        grid=(rows.m // rows.tm, d_ff // TF_MLP),
        in_specs=[
            pl.BlockSpec((rows.tm, D_MODEL), lambda i, f: (i, 0)),
            pl.BlockSpec((1, D_MODEL), lambda i, f: (0, 0)),
            _mod_spec(rows, mods, 4, D_MODEL, full),
            _mod_spec(rows, mods, 3, D_MODEL, full),
            _mod_spec(rows, mods, 5, D_MODEL, full),
            pl.BlockSpec((None, D_MODEL, TF_MLP), lambda i, f: (layer, 0, f)),
            pl.BlockSpec((None, TF_MLP, D_MODEL), lambda i, f: (layer, f, 0)),
        ],
        out_specs=pl.BlockSpec((rows.tm, D_MODEL), lambda i, f: (i, 0)),
        scratch_shapes=[pltpu.VMEM((rows.tm, D_MODEL), BF16)],
        compiler_params=_cparams("parallel", "arbitrary"),
        name="sqrelu_mlp",
    )(x, norm_g.reshape(1, D_MODEL), mods, mods, mods, w1, w2)


def _proj_kernel(*refs, qk_norm, n_out):
    if qk_norm:
        x_ref, g_ref, sc_ref, sh_ref, w_ref, ng_ref, cos_ref, sa_ref, sb_ref = refs[:9]
        rest = refs[9:]
    else:
        x_ref, g_ref, sc_ref, sh_ref, w_ref = refs[:5]
        rest = refs[5:]
    out_refs, h_ref = rest[:n_out], rest[n_out]

    @pl.when(pl.program_id(1) == 0)
    def _():
        _norm_mod_to_bf16(x_ref, g_ref, sc_ref, sh_ref, h_ref)

    acc = _dot(h_ref[...], w_ref[...].astype(BF16))
    tn = acc.shape[1]
    if not qk_norm:
        for o_ref in out_refs:
            o_ref[...] = acc.astype(o_ref.dtype)
        return

    cos, sa, sb, ng = cos_ref[...], sa_ref[...], sb_ref[...], ng_ref[...]
    for grp in range(tn // HEAD_DIM):
        cols = slice(grp * HEAD_DIM, (grp + 1) * HEAD_DIM)
        a = acc[:, cols]
        y = a * lax.rsqrt(jnp.mean(a * a, axis=-1, keepdims=True) + EPS) * ng
        y = (y * cos + pltpu.roll(y, HEAD_DIM - ROT_DIM // 2, 1) * sa
             + pltpu.roll(y, ROT_DIM // 2, 1) * sb)
        for o_ref in out_refs:
            o_ref[:, cols] = y.astype(o_ref.dtype)


def _proj(rows, x, mods, k_shift, k_scale, norm_g, w, layer, out_dtypes, rope=None, head_g=None):
    n = w.shape[2]
    full = lambda j: 0
    qk_norm = rope is not None
    in_specs = [
        pl.BlockSpec((rows.tm, D_MODEL), lambda i, j: (i, 0)),
        pl.BlockSpec((1, D_MODEL), lambda i, j: (0, 0)),
        _mod_spec(rows, mods, k_scale, D_MODEL, full),
        _mod_spec(rows, mods, k_shift, D_MODEL, full),
        pl.BlockSpec((None, D_MODEL, TN_PROJ), lambda i, j: (layer, 0, j)),
    ]
    args = [x, norm_g.reshape(1, D_MODEL), mods, mods, w]
    if qk_norm:
        tps = rows.tiles_per_seq
        in_specs.append(pl.BlockSpec((1, HEAD_DIM), lambda i, j: (0, 0)))
        args.append(head_g.reshape(1, HEAD_DIM))
        for t in rope:
            in_specs.append(pl.BlockSpec((rows.tm, HEAD_DIM), lambda i, j: (i % tps, 0)))
            args.append(t)
    outs = pl.pallas_call(
        functools.partial(_proj_kernel, qk_norm=qk_norm, n_out=len(out_dtypes)),
        out_shape=[jax.ShapeDtypeStruct((rows.m, n), dt) for dt in out_dtypes],
        grid=(rows.m // rows.tm, n // TN_PROJ),
        in_specs=in_specs,
        out_specs=[pl.BlockSpec((rows.tm, TN_PROJ), lambda i, j: (i, j)) for _ in out_dtypes],
        scratch_shapes=[pltpu.VMEM((rows.tm, D_MODEL), BF16)],
        compiler_params=_cparams("parallel", "arbitrary"),
        name="norm_proj_rope" if qk_norm else "norm_proj",
    )(*args)
    return outs


def _res_linear_kernel(a_ref, w_ref, x_ref, gate_ref, o_ref):
    out = _dot(a_ref[...].astype(BF16), w_ref[...].astype(BF16))
    o_ref[...] = x_ref[...] + gate_ref[...] * out


def _res_linear(rows, a, w, layer, x, mods, k_gate):
    _, kdim, n = w.shape
    return pl.pallas_call(
        _res_linear_kernel,
        out_shape=jax.ShapeDtypeStruct((rows.m, n), F32),
        grid=(rows.m // rows.tm, n // TN_PROJ),
        in_specs=[
            pl.BlockSpec((rows.tm, kdim), lambda i, j: (i, 0)),
            pl.BlockSpec((None, kdim, TN_PROJ), lambda i, j: (layer, 0, j)),
            pl.BlockSpec((rows.tm, TN_PROJ), lambda i, j: (i, j)),
            _mod_spec(rows, mods, k_gate, TN_PROJ, lambda j: j),
        ],
        out_specs=pl.BlockSpec((rows.tm, TN_PROJ), lambda i, j: (i, j)),
        compiler_params=_cparams("parallel", "arbitrary"),
        name="out_proj_residual",
    )(a, w, x, mods)


def _lambda_full(lq1_ref, lk1_ref, lq2_ref, lk2_ref, lam_init):
    d1 = jnp.sum(lq1_ref[...] * lk1_ref[...], axis=-1, keepdims=True)
    d2 = jnp.sum(lq2_ref[...] * lk2_ref[...], axis=-1, keepdims=True)
    return jnp.exp(d1) - jnp.exp(d2) + lam_init


def _subln(o, g, lam_init):
    y = o * lax.rsqrt(jnp.mean(o * o, axis=-1, keepdims=True) + EPS)
    return y * g * (1.0 - lam_init)


def _attn_prompt_kernel(q_ref, k_ref, v_ref, lq1_ref, lk1_ref, lq2_ref, lk2_ref, sg_ref,
                        o_ref, m_ref, l_ref, acc_ref, *, lam_init):
    tq = q_ref.shape[0]
    i = pl.program_id(2)
    m_ref[...] = jnp.full(m_ref.shape, NEG_BIG, F32)
    l_ref[...] = jnp.zeros(l_ref.shape, F32)
    acc_ref[...] = jnp.zeros(acc_ref.shape, F32)

    def chunk(c, masked):
        k0 = pl.multiple_of(c * tq, tq)
        vc = v_ref[pl.ds(k0, tq), :]
        for comp in range(2):
            cols = slice(comp * HEAD_DIM, (comp + 1) * HEAD_DIM)
            s = lax.dot_general(q_ref[:, cols], k_ref[pl.ds(k0, tq), cols],
                                (((1,), (1,)), ((), ())), preferred_element_type=F32) * SCALE
            if masked:
                row = lax.broadcasted_iota(jnp.int32, s.shape, 0)
                col = lax.broadcasted_iota(jnp.int32, s.shape, 1)
                s = jnp.where(col <= row, s, NEG_BIG)
            m_old = m_ref[comp]
            m_new = jnp.maximum(m_old, jnp.max(s, axis=-1, keepdims=True))
            p = jnp.exp(s - m_new)
            alpha = jnp.exp(m_old - m_new)
            l_ref[comp] = alpha * l_ref[comp] + jnp.sum(p, axis=-1, keepdims=True)
            acc_ref[comp] = alpha * acc_ref[comp] + _dot(p.astype(BF16), vc)
            m_ref[comp] = m_new

    def full_chunk(c, carry):
        chunk(c, False)
        return carry

    lax.fori_loop(0, i, full_chunk, 0)
    chunk(i, True)

    lam = _lambda_full(lq1_ref, lk1_ref, lq2_ref, lk2_ref, lam_init)
    o = acc_ref[0] / l_ref[0] - lam * (acc_ref[1] / l_ref[1])
    o_ref[...] = _subln(o, sg_ref[...], lam_init).astype(o_ref.dtype)


def _attn_prompt(q, k, v, lam_params, subln_g, lam_init, n_seq, seq):
    nq = seq // TQ_ATTN
    lam_spec = pl.BlockSpec((1, HEAD_DIM), lambda b, h, i: (0, 0))
    return pl.pallas_call(
        functools.partial(_attn_prompt_kernel, lam_init=lam_init),
        out_shape=jax.ShapeDtypeStruct(q.shape, BF16),
        grid=(n_seq, N_HEADS, nq),
        in_specs=[
            pl.BlockSpec((TQ_ATTN, V_DIM), lambda b, h, i: (b * nq + i, h)),
            pl.BlockSpec((seq, V_DIM), lambda b, h, i: (b, h)),
            pl.BlockSpec((seq, V_DIM), lambda b, h, i: (b, h)),
            lam_spec, lam_spec, lam_spec, lam_spec,
            pl.BlockSpec((1, V_DIM), lambda b, h, i: (0, 0)),
        ],
        out_specs=pl.BlockSpec((TQ_ATTN, V_DIM), lambda b, h, i: (b * nq + i, h)),
        scratch_shapes=[
            pltpu.VMEM((2, TQ_ATTN, 1), F32),
            pltpu.VMEM((2, TQ_ATTN, 1), F32),
            pltpu.VMEM((2, TQ_ATTN, V_DIM), F32),
        ],
        compiler_params=_cparams("parallel", "parallel", "arbitrary"),
        name="diff_attn_prompt",
    )(q, k, v, *[p.reshape(1, HEAD_DIM) for p in lam_params], subln_g.reshape(1, V_DIM))


def _attn_sample_kernel(pt_ref, q_ref, kn_ref, vn_ref, lq1_ref, lk1_ref, lq2_ref, lk2_ref, sg_ref,
                        *refs, lam_init):
    npg = PAGES_PER_STEP
    k_refs, v_refs = refs[:npg], refs[npg:2 * npg]
    o_ref, m_ref, l_ref, acc_ref = refs[2 * npg:]
    c = pl.program_id(1)
    nr = 2 * N_HEADS
    flat = k_refs[0].shape[0]

    @pl.when(c == 0)
    def _():
        m_ref[...] = jnp.full(m_ref.shape, NEG_BIG, F32)
        l_ref[...] = jnp.zeros(l_ref.shape, F32)
        acc_ref[...] = jnp.zeros(acc_ref.shape, F32)

    q = q_ref[...]
    q_bf = q.astype(BF16)
    row = lax.broadcasted_iota(jnp.int32, (nr, flat), 0)
    lane = lax.broadcasted_iota(jnp.int32, (nr, flat), 1)
    own = (N_HEADS - 1 - jnp.bitwise_and(row, N_HEADS - 1)) * 2 + jnp.right_shift(row, 3)
    valid = jnp.bitwise_and(lane, nr - 1) == own

    def align_to_values(pr):
        blocks = []
        for half in range(2):
            for comp in range(2):
                base = (half * N_HEADS - (N_HEADS - 1) - comp) % LANE
                tiles = [pltpu.roll(pr[comp * N_HEADS:(comp + 1) * N_HEADS, t * LANE:(t + 1) * LANE],
                                    base, 1, stride=1, stride_axis=0) for t in range(flat // LANE)]
                blocks.append(jnp.concatenate(tiles, axis=1))
        return jnp.concatenate(blocks, axis=0)

    m_old = m_ref[...]
    m_new = m_old
    s_pages = []
    for p in range(npg):
        s = lax.dot_general(q_bf, k_refs[p][...].astype(BF16), (((1,), (1,)), ((), ())),
                            preferred_element_type=F32) * SCALE
        s = jnp.where(valid, s, NEG_BIG)
        s_pages.append(s)
        m_new = jnp.maximum(m_new, jnp.max(s, axis=-1, keepdims=True))
    alpha = jnp.exp(m_old - m_new)
    l_new = alpha * l_ref[...]
    pv = jnp.zeros(acc_ref.shape, F32)
    for p in range(npg):
        pr = jnp.exp(s_pages[p] - m_new)
        l_new = l_new + jnp.sum(pr, axis=-1, keepdims=True)
        pv = pv + _dot(align_to_values(pr).astype(BF16), v_refs[p][...].astype(BF16))
    acc_ref[...] = jnp.concatenate([alpha, alpha], axis=0) * acc_ref[...] + pv
    l_ref[...] = l_new
    m_ref[...] = m_new

    @pl.when(c == pl.num_programs(1) - 1)
    def _():
        s_new = jnp.sum(q * kn_ref[...], axis=-1, keepdims=True) * SCALE
        m_fin = jnp.maximum(m_ref[...], s_new)
        a = jnp.exp(m_ref[...] - m_fin)
        p_new = jnp.exp(s_new - m_fin)
        l_fin = a * l_ref[...] + p_new
        two = lambda x: jnp.concatenate([x, x], axis=0)
        w_all = (two(a) * acc_ref[...] + two(p_new) * vn_ref[...]) / two(l_fin)
        lam = _lambda_full(lq1_ref, lk1_ref, lq2_ref, lk2_ref, lam_init)
        halves = [w_all[half * nr:half * nr + N_HEADS] - lam * w_all[half * nr + N_HEADS:(half + 1) * nr]
                  for half in range(2)]
        ms = sum(jnp.sum(d * d, axis=-1, keepdims=True) for d in halves) * (1.0 / V_DIM)
        inv = lax.rsqrt(ms + EPS) * (1.0 - lam_init)
        for half in range(2):
            o_ref[half * N_HEADS:(half + 1) * N_HEADS, :] = (
                halves[half] * inv * sg_ref[:, half * LANE:(half + 1) * LANE])


def _attn_sample(q, k_new, v_new, cache_k, cache_v, page_table, lam_params, subln_g, lam_init):
    n_seq, n_pages = page_table.shape
    n_pool = cache_k.shape[0]
    nr = 2 * N_HEADS
    flat = PAGE_SIZE * nr
    npg = PAGES_PER_STEP
    n_chunks = n_pages // npg

    k_flat = cache_k.reshape(n_pool, flat, HEAD_DIM)
    v_flat = cache_v.reshape(n_pool, PAGE_SIZE, N_HEADS, 2, LANE).transpose(0, 1, 3, 2, 4).reshape(n_pool, flat, LANE)

    def qk_rows(x):
        return x.reshape(n_seq, N_HEADS, 2, HEAD_DIM).transpose(0, 2, 1, 3)[:, :, ::-1].reshape(n_seq, nr, HEAD_DIM)

    v_rows = v_new.reshape(n_seq, N_HEADS, 2, LANE).transpose(0, 2, 1, 3)[:, :, ::-1]
    v_rows = jnp.broadcast_to(v_rows[:, :, None], (n_seq, 2, 2, N_HEADS, LANE)).reshape(n_seq, 2 * nr, LANE)

    row_spec = pl.BlockSpec((None, nr, HEAD_DIM), lambda b, c, pt: (b, 0, 0))
    lam_spec = pl.BlockSpec((1, HEAD_DIM), lambda b, c, pt: (0, 0))

    def page_spec(p):
        return pl.BlockSpec((None, flat, LANE), lambda b, c, pt: (pt[b * n_pages + c * npg + p], 0, 0))

    grid_spec = pltpu.PrefetchScalarGridSpec(
        num_scalar_prefetch=1,
        grid=(n_seq, n_chunks),
        in_specs=[row_spec, row_spec, pl.BlockSpec((None, 2 * nr, LANE), lambda b, c, pt: (b, 0, 0)),
                  lam_spec, lam_spec, lam_spec, lam_spec,
                  pl.BlockSpec((1, V_DIM), lambda b, c, pt: (0, 0))]
                 + [page_spec(p) for p in range(npg)] + [page_spec(p) for p in range(npg)],
        out_specs=row_spec,
        scratch_shapes=[
            pltpu.VMEM((nr, 1), F32),
            pltpu.VMEM((nr, 1), F32),
            pltpu.VMEM((2 * nr, LANE), F32),
        ],
    )
    out = pl.pallas_call(
        functools.partial(_attn_sample_kernel, lam_init=lam_init),
        out_shape=jax.ShapeDtypeStruct((n_seq, nr, LANE), F32),
        grid_spec=grid_spec,
        compiler_params=_cparams("parallel", "arbitrary"),
        name="diff_attn_sample",
    )(page_table.reshape(-1), qk_rows(q), qk_rows(k_new), v_rows,
      *[p.reshape(1, HEAD_DIM) for p in lam_params], subln_g.reshape(1, V_DIM),
      *([k_flat] * npg), *([v_flat] * npg))
    return out.reshape(n_seq, 2, N_HEADS, LANE)[:, :, ::-1].transpose(0, 2, 1, 3).reshape(n_seq, N_HEADS * V_DIM)


def _rope_tables(pos):
    half = ROT_DIM // 2
    inv_freq = jnp.power(ROPE_THETA, -jnp.arange(half, dtype=F32) * 2.0 / ROT_DIM)
    ang = pos.astype(F32)[:, None] * inv_freq[None, :]
    cos, sin = jnp.cos(ang), jnp.sin(ang)
    n = pos.shape[0]
    ones = jnp.ones((n, HEAD_DIM - ROT_DIM), F32)
    zeros_tail = jnp.zeros((n, HEAD_DIM - half), F32)
    cos_t = jnp.concatenate([cos, cos, ones], axis=1)
    sin_a = jnp.concatenate([-sin, zeros_tail], axis=1)
    sin_b = jnp.concatenate([jnp.zeros((n, half), F32), sin, jnp.zeros((n, HEAD_DIM - ROT_DIM), F32)], axis=1)
    return cos_t, sin_a, sin_b


def _lam_init(layer):
    return 0.8 - 0.6 * math.exp(-0.3 * layer)


def kernel(x_prompt, x_sample, state_conv, cache_k, cache_v, page_table, c_prompt, c_sample,
           ada_w, ada_b, norm_mix_g, norm_ff_g, w_ff1, w_ff2,
           conv_w_pw1, conv_b_pw1, conv_w_dw, conv_b_dw, conv_ln_g, conv_ln_b, conv_w_pw2, conv_b_pw2,
           kv_ada_w, kv_ada_b, kv_norm_g, w_k, w_v, k_norm_g,
           w_q, q_norm_g, lambda_q1, lambda_k1, lambda_q2, lambda_k2, subln_g, w_o):
    n_seq, seq, _ = x_prompt.shape
    n_dec = x_sample.shape[0]
    assert x_sample.shape[1] == 1 and n_dec <= SAMPLE_ROWS
    assert seq % TM_PROMPT == 0 and seq % TQ_ATTN == 0
    n_pages = page_table.shape[1]
    assert n_pages % PAGES_PER_STEP == 0
    n_past = n_pages * PAGE_SIZE
    width = N_HEADS * V_DIM

    rows_p = Rows(m=n_seq * seq, tm=TM_PROMPT, tiles_per_seq=seq // TM_PROMPT, per_row=False)
    rows_s = Rows(m=SAMPLE_ROWS, tm=SAMPLE_ROWS, tiles_per_seq=1, per_row=True)
    pad_s = SAMPLE_ROWS - n_dec

    c_all = jnp.concatenate([c_sample, c_prompt,
                             jnp.zeros((SAMPLE_ROWS - n_dec - n_seq, D_MODEL), F32)], axis=0)
    mods_all = _ada(c_all, ada_w, ada_b)
    kv_mods_all = _ada(c_all, kv_ada_w[None], kv_ada_b[None])[0]

    def group_mods(m, rows):
        if rows.per_row:
            return m
        return m[n_dec:n_dec + n_seq].reshape(n_seq, 1, m.shape[-1])

    def trunk(rows, x, rope, conv_fn, attn_fn):
        new_glu = []
        kv_mods = group_mods(kv_mods_all, rows)
        k_f32 = v_f32 = k_att = v_att = None
        for l in range(DEPTH):
            mods = group_mods(mods_all[l], rows)
            if l == N_A_LAYERS:
                k_f32, k_bf = _proj(rows, x, kv_mods, 0, 1, kv_norm_g, w_k[None], 0, (F32, BF16),
                                    rope=rope, head_g=k_norm_g)
                v_f32, v_bf = _proj(rows, x, kv_mods, 0, 1, kv_norm_g, w_v[None], 0, (F32, BF16))
                k_att, v_att = (k_f32, v_f32) if rows.per_row else (k_bf, v_bf)
            if l < N_A_LAYERS:
                glu = _pw1_glu(rows, x, mods, norm_mix_g[l], conv_w_pw1, l, conv_b_pw1[l])
                new_glu.append(glu)
                x = conv_fn(l, glu, x, mods)
            else:
                j = l - N_A_LAYERS
                q_dtype = F32 if rows.per_row else BF16
                (q,) = _proj(rows, x, mods, 0, 1, norm_mix_g[l], w_q, j, (q_dtype,),
                             rope=rope, head_g=q_norm_g[j])
                lam_params = (lambda_q1[j], lambda_k1[j], lambda_q2[j], lambda_k2[j])
                o = attn_fn(q, k_att, v_att, lam_params, subln_g[j], _lam_init(l))
                x = _res_linear(rows, o, w_o, j, x, mods, 2)
            x = _mlp(rows, x, mods, norm_ff_g[l], w_ff1, w_ff2, l)
        return x, new_glu, k_f32, v_f32

    rope_p = _rope_tables(jnp.arange(seq))
    state_p = jnp.zeros((n_seq, CONV_HALO, D_MODEL), F32)

    def conv_prompt(l, glu, x, mods):
        return _conv_pw2_prompt(rows_p, glu, state_p, x, mods, conv_w_dw[l], conv_b_dw[l],
                                conv_ln_g[l], conv_ln_b[l], conv_w_pw2, l, conv_b_pw2[l])

    def attn_prompt(q, k, v, lam_params, sg, lam_init):
        return _attn_prompt(q, k, v, lam_params, sg, lam_init, n_seq, seq)

    y_p, glu_p, k_p, v_p = trunk(rows_p, x_prompt.reshape(n_seq * seq, D_MODEL), rope_p,
                                 conv_prompt, attn_prompt)
    conv_p = jnp.stack([g.reshape(n_seq, seq, D_MODEL)[:, seq - (CONV_WIDTH - 1):] for g in glu_p])

    rope_s = _rope_tables(jnp.full((SAMPLE_ROWS,), n_past, jnp.int32))
    conv_s_rows = []

    def conv_sample(l, glu, x, mods):
        hist = jnp.pad(state_conv[l].transpose(1, 0, 2), ((0, 0), (0, pad_s), (0, 0)))
        full = jnp.concatenate([hist, glu[None]], axis=0)
        conv_s_rows.append(full[1:, :n_dec].transpose(1, 0, 2))
        return _conv_pw2_sample(rows_s, full, x, mods, conv_w_dw[l], conv_b_dw[l],
                                conv_ln_g[l], conv_ln_b[l], conv_w_pw2, l, conv_b_pw2[l])

    def attn_sample(q, k, v, lam_params, sg, lam_init):
        o = _attn_sample(q[:n_dec], k[:n_dec], v[:n_dec], cache_k, cache_v, page_table,
                         lam_params, sg, lam_init)
        return jnp.pad(o, ((0, pad_s), (0, 0)))

    x_s = jnp.pad(x_sample.reshape(n_dec, D_MODEL), ((0, pad_s), (0, 0)))
    y_s, _, k_s, v_s = trunk(rows_s, x_s, rope_s, conv_sample, attn_sample)

    return (y_p.reshape(n_seq, seq, D_MODEL),
            y_s[:n_dec].reshape(n_dec, 1, D_MODEL),
            conv_p,
            jnp.stack(conv_s_rows),
            k_p.reshape(n_seq, seq, N_HEADS, 2, HEAD_DIM),
            v_p.reshape(n_seq, seq, N_HEADS, V_DIM),
            k_s[:n_dec].reshape(n_dec, 1, N_HEADS, 2, HEAD_DIM),
            v_s[:n_dec].reshape(n_dec, 1, N_HEADS, V_DIM))
```

```python
import functools
import math
from typing import NamedTuple

import jax
import jax.numpy as jnp
from jax import lax
from jax.experimental import pallas as pl
from jax.experimental.pallas import tpu as pltpu

F32 = jnp.float32
BF16 = jnp.bfloat16

D_MODEL = 2048
DEPTH = 4
N_A_LAYERS = DEPTH // 2
CONV_WIDTH = 31
HEAD_DIM = 128
V_DIM = 2 * HEAD_DIM
N_HEADS = D_MODEL // V_DIM
ROT_DIM = HEAD_DIM // 4
ROPE_THETA = 500000.0
EPS = 1e-6
SCALE = HEAD_DIM ** -0.5
PAGE_SIZE = 128
NEG_BIG = -1e30

LANE = 128
SUBLANE = 8
VMEM_LIMIT_BYTES = 58 * 1024 * 1024

SAMPLE_ROWS = 16
TM_PROMPT = 512
TN_PROJ = 1024
TN_GLU = 512
TF_MLP = 2048
MLP_SPLIT = 4
TN_ADA = 1024
NORM_ROWS = 64
CONV_COLS = 256
CONV_ROWS = 64
CONV_HALO = 32
TQ_ATTN = 1024
TK_ATTN = 512
ATTN_SUB_ROWS = 512
PAGES_PER_STEP = 8


class Rows(NamedTuple):
    m: int
    tm: int
    tiles_per_seq: int
    per_row: bool


def _cparams(*sem):
    return pltpu.CompilerParams(dimension_semantics=sem, vmem_limit_bytes=VMEM_LIMIT_BYTES)


def _mod_spec(rows, mods, k, width, col_of_j):
    nb = D_MODEL // width
    if rows.per_row:
        return pl.BlockSpec((rows.tm, width), lambda i, j: (0, k * nb + col_of_j(j)))
    return pl.BlockSpec((None, 1, width), lambda i, j: (i // rows.tiles_per_seq, 0, k * nb + col_of_j(j)))


def _rows_of(ref, r0, n):
    if ref.shape[0] == 1:
        return ref[...]
    return ref[pl.ds(r0, n), :]


def _norm_mod_to_bf16(x_ref, g_ref, sc_ref, sh_ref, h_ref):
    tm = x_ref.shape[0]
    rb = min(tm, NORM_ROWS)

    def body(r, carry):
        r0 = pl.multiple_of(r * rb, rb)
        x = x_ref[pl.ds(r0, rb), :]
        y = x * lax.rsqrt(jnp.mean(x * x, axis=-1, keepdims=True) + EPS)
        y = y * g_ref[...]
        h = y * (1.0 + _rows_of(sc_ref, r0, rb)) + _rows_of(sh_ref, r0, rb)
        h_ref[pl.ds(r0, rb), :] = h.astype(BF16)
        return carry

    lax.fori_loop(0, tm // rb, body, 0)


def _dot(a, b):
    return jnp.dot(a, b, preferred_element_type=F32)


def _ada_kernel(c_ref, w_ref, b_ref, o_ref):
    c = c_ref[...]
    s = (c * jax.nn.sigmoid(c)).astype(BF16)
    o_ref[...] = _dot(s, w_ref[...].astype(BF16)) + b_ref[...]


def _ada(c_all, w, b):
    n_layers, _, n = w.shape
    r = c_all.shape[0]
    return pl.pallas_call(
        _ada_kernel,
        out_shape=jax.ShapeDtypeStruct((n_layers, r, n), F32),
        grid=(n_layers, n // TN_ADA),
        in_specs=[
            pl.BlockSpec((r, D_MODEL), lambda l, j: (0, 0)),
            pl.BlockSpec((None, D_MODEL, TN_ADA), lambda l, j: (l, 0, j)),
            pl.BlockSpec((None, 1, TN_ADA), lambda l, j: (l, 0, j)),
        ],
        out_specs=pl.BlockSpec((None, r, TN_ADA), lambda l, j: (l, 0, j)),
        compiler_params=_cparams("parallel", "parallel"),
        name="ada_mod",
    )(c_all, w, b.reshape(n_layers, 1, n))


def _pw1_glu_kernel(x_ref, g_ref, sc_ref, sh_ref, wa_ref, wg_ref, ba_ref, bg_ref, o_ref, h_ref):
    @pl.when(pl.program_id(1) == 0)
    def _():
        _norm_mod_to_bf16(x_ref, g_ref, sc_ref, sh_ref, h_ref)

    h = h_ref[...]
    a = _dot(h, wa_ref[...].astype(BF16)) + ba_ref[...]
    gt = _dot(h, wg_ref[...].astype(BF16)) + bg_ref[...]
    o_ref[...] = a * jax.nn.sigmoid(gt)


def _pw1_glu(rows, x, mods, norm_g, w, layer, b):
    nj = D_MODEL // TN_GLU
    b2 = b.reshape(1, 2 * D_MODEL)
    full = lambda j: 0
    return pl.pallas_call(
        _pw1_glu_kernel,
        out_shape=jax.ShapeDtypeStruct((rows.m, D_MODEL), F32),
        grid=(rows.m // rows.tm, nj),
        in_specs=[
            pl.BlockSpec((rows.tm, D_MODEL), lambda i, j: (i, 0)),
            pl.BlockSpec((1, D_MODEL), lambda i, j: (0, 0)),
            _mod_spec(rows, mods, 1, D_MODEL, full),
            _mod_spec(rows, mods, 0, D_MODEL, full),
            pl.BlockSpec((None, D_MODEL, TN_GLU), lambda i, j: (layer, 0, j)),
            pl.BlockSpec((None, D_MODEL, TN_GLU), lambda i, j: (layer, 0, j + nj)),
            pl.BlockSpec((1, TN_GLU), lambda i, j: (0, j)),
            pl.BlockSpec((1, TN_GLU), lambda i, j: (0, j + nj)),
        ],
        out_specs=pl.BlockSpec((rows.tm, TN_GLU), lambda i, j: (i, j)),
        scratch_shapes=[pltpu.VMEM((rows.tm, D_MODEL), BF16)],
        compiler_params=_cparams("parallel", "arbitrary"),
        name="pw1_glu",
    )(x, norm_g.reshape(1, D_MODEL), mods, mods, w, w, b2, b2)


def _ln_silu_to_bf16(yc_ref, lng_ref, lnb_ref, y_ref):
    ncb, tm, cw = yc_ref.shape
    rb = min(tm, NORM_ROWS)

    def body(r, carry):
        r0 = pl.multiple_of(r * rb, rb)
        parts = [yc_ref[cb, pl.ds(r0, rb), :] for cb in range(ncb)]
        mu = sum(jnp.sum(p, axis=-1, keepdims=True) for p in parts) * (1.0 / D_MODEL)
        var = sum(jnp.sum(jnp.square(p - mu), axis=-1, keepdims=True) for p in parts) * (1.0 / D_MODEL)
        inv = lax.rsqrt(var + EPS)
        for cb in range(ncb):
            cols = slice(cb * cw, (cb + 1) * cw)
            y = (parts[cb] - mu) * inv * lng_ref[:, cols] + lnb_ref[:, cols]
            y = y * jax.nn.sigmoid(y)
            y_ref[pl.ds(r0, rb), cols] = y.astype(BF16)
        return carry

    lax.fori_loop(0, tm // rb, body, 0)


def _pw2_residual(y_ref, w_ref, b_ref, x_ref, gate_ref, o_ref):
    out = _dot(y_ref[...], w_ref[...].astype(BF16)) + b_ref[...]
    o_ref[...] = x_ref[...] + gate_ref[...] * out


def _conv_prompt_kernel(glu_ref, halo_ref, st_ref, wd_ref, bd_ref, lng_ref, lnb_ref,
                        w_ref, b_ref, x_ref, gate_ref, o_ref, win_ref, shf_ref, yc_ref, y_ref, *, tiles_per_seq):
    ncb, win_rows, cw = win_ref.shape
    tm = glu_ref.shape[0]

    @pl.when(pl.program_id(1) == 0)
    def _():
        first = pl.program_id(0) % tiles_per_seq == 0
        for cb in range(ncb):
            cols = slice(cb * cw, (cb + 1) * cw)
            win_ref[cb, 0:CONV_HALO, :] = jnp.where(first, st_ref[:, cols], halo_ref[:, cols])
            win_ref[cb, CONV_HALO:, :] = glu_ref[:, cols]

        base = CONV_HALO - (CONV_WIDTH - 1)

        def conv_cols(cb, carry):
            wd = wd_ref[cb]
            for r in range(1, SUBLANE):
                shf_ref[r - 1, 0:win_rows - SUBLANE, :] = win_ref[cb, r:r + win_rows - SUBLANE, :]
            for rb in range(tm // CONV_ROWS):
                acc = jnp.zeros((CONV_ROWS, cw), F32)
                for w in range(CONV_WIDTH):
                    r = (base + w) % SUBLANE
                    start = rb * CONV_ROWS + (base + w) - r
                    if r == 0:
                        tap = win_ref[cb, pl.ds(start, CONV_ROWS), :]
                    else:
                        tap = shf_ref[r - 1, pl.ds(start, CONV_ROWS), :]
                    acc = acc + tap * wd[w:w + 1, :]
                yc_ref[cb, pl.ds(rb * CONV_ROWS, CONV_ROWS), :] = acc + bd_ref[cb]
            return carry

        lax.fori_loop(0, ncb, conv_cols, 0)
        _ln_silu_to_bf16(yc_ref, lng_ref, lnb_ref, y_ref)

    _pw2_residual(y_ref, w_ref, b_ref, x_ref, gate_ref, o_ref)


def _conv_sample_kernel(full_ref, wd_ref, bd_ref, lng_ref, lnb_ref,
                        w_ref, b_ref, x_ref, gate_ref, o_ref, yc_ref, y_ref):
    ncb, _, cw = yc_ref.shape

    @pl.when(pl.program_id(1) == 0)
    def _():
        for cb in range(ncb):
            cols = slice(cb * cw, (cb + 1) * cw)
            wd = wd_ref[cb]
            acc = jnp.zeros((full_ref.shape[1], cw), F32)
            for w in range(CONV_WIDTH):
                acc = acc + full_ref[w, :, cols] * wd[w:w + 1, :]
            yc_ref[cb] = acc + bd_ref[cb]
        _ln_silu_to_bf16(yc_ref, lng_ref, lnb_ref, y_ref)

    _pw2_residual(y_ref, w_ref, b_ref, x_ref, gate_ref, o_ref)


def _conv_weights(w_dw, b_dw):
    ncb = D_MODEL // CONV_COLS
    wd = jnp.pad(w_dw, ((0, CONV_HALO - CONV_WIDTH), (0, 0)))
    wd = wd.reshape(CONV_HALO, ncb, CONV_COLS).transpose(1, 0, 2)
    return wd, b_dw.reshape(ncb, 1, CONV_COLS)


def _conv_pw2_prompt(rows, glu, state, x, mods, w_dw, b_dw, ln_g, ln_b, w, layer, b):
    ncb = D_MODEL // CONV_COLS
    nj = D_MODEL // TN_PROJ
    wd, bd = _conv_weights(w_dw, b_dw)
    halo_per_tile = rows.tm // CONV_HALO
    tps = rows.tiles_per_seq
    const2 = lambda i, j: (0, 0)
    const3 = lambda i, j: (0, 0, 0)
    return pl.pallas_call(
        functools.partial(_conv_prompt_kernel, tiles_per_seq=tps),
        out_shape=jax.ShapeDtypeStruct((rows.m, D_MODEL), F32),
        grid=(rows.m // rows.tm, nj),
        in_specs=[
            pl.BlockSpec((rows.tm, D_MODEL), lambda i, j: (i, 0)),
            pl.BlockSpec((CONV_HALO, D_MODEL), lambda i, j: (jnp.maximum(i * halo_per_tile - 1, 0), 0)),
            pl.BlockSpec((None, CONV_HALO, D_MODEL), lambda i, j: (i // tps, 0, 0)),
            pl.BlockSpec((ncb, CONV_HALO, CONV_COLS), const3),
            pl.BlockSpec((ncb, 1, CONV_COLS), const3),
            pl.BlockSpec((1, D_MODEL), const2),
            pl.BlockSpec((1, D_MODEL), const2),
            pl.BlockSpec((None, D_MODEL, TN_PROJ), lambda i, j: (layer, 0, j)),
            pl.BlockSpec((1, TN_PROJ), lambda i, j: (0, j)),
            pl.BlockSpec((rows.tm, TN_PROJ), lambda i, j: (i, j)),
            _mod_spec(rows, mods, 2, TN_PROJ, lambda j: j),
        ],
        out_specs=pl.BlockSpec((rows.tm, TN_PROJ), lambda i, j: (i, j)),
        scratch_shapes=[
            pltpu.VMEM((ncb, CONV_HALO + rows.tm, CONV_COLS), F32),
            pltpu.VMEM((SUBLANE - 1, CONV_HALO + rows.tm, CONV_COLS), F32),
            pltpu.VMEM((ncb, rows.tm, CONV_COLS), F32),
            pltpu.VMEM((rows.tm, D_MODEL), BF16),
        ],
        compiler_params=_cparams("parallel", "arbitrary"),
        name="conv_pw2_prompt",
    )(glu, glu, state, wd, bd, ln_g.reshape(1, D_MODEL), ln_b.reshape(1, D_MODEL),
      w, b.reshape(1, D_MODEL), x, mods)


def _conv_pw2_sample(rows, full, x, mods, w_dw, b_dw, ln_g, ln_b, w, layer, b):
    ncb = D_MODEL // CONV_COLS
    nj = D_MODEL // TN_PROJ
    wd, bd = _conv_weights(w_dw, b_dw)
    const2 = lambda i, j: (0, 0)
    const3 = lambda i, j: (0, 0, 0)
    return pl.pallas_call(
        _conv_sample_kernel,
        out_shape=jax.ShapeDtypeStruct((rows.m, D_MODEL), F32),
        grid=(rows.m // rows.tm, nj),
        in_specs=[
            pl.BlockSpec((CONV_WIDTH, rows.tm, D_MODEL), lambda i, j: (0, i, 0)),
            pl.BlockSpec((ncb, CONV_HALO, CONV_COLS), const3),
            pl.BlockSpec((ncb, 1, CONV_COLS), const3),
            pl.BlockSpec((1, D_MODEL), const2),
            pl.BlockSpec((1, D_MODEL), const2),
            pl.BlockSpec((None, D_MODEL, TN_PROJ), lambda i, j: (layer, 0, j)),
            pl.BlockSpec((1, TN_PROJ), lambda i, j: (0, j)),
            pl.BlockSpec((rows.tm, TN_PROJ), lambda i, j: (i, j)),
            _mod_spec(rows, mods, 2, TN_PROJ, lambda j: j),
        ],
        out_specs=pl.BlockSpec((rows.tm, TN_PROJ), lambda i, j: (i, j)),
        scratch_shapes=[
            pltpu.VMEM((ncb, rows.tm, CONV_COLS), F32),
            pltpu.VMEM((rows.tm, D_MODEL), BF16),
        ],
        compiler_params=_cparams("parallel", "arbitrary"),
        name="conv_pw2_sample",
    )(full, wd, bd, ln_g.reshape(1, D_MODEL), ln_b.reshape(1, D_MODEL),
      w, b.reshape(1, D_MODEL), x, mods)


def _mlp_kernel(x_ref, g_ref, sc_ref, sh_ref, gate_ref, w1_ref, w2_ref, o_ref, h_ref):
    f = pl.program_id(1)

    @pl.when(f == 0)
    def _():
        _norm_mod_to_bf16(x_ref, g_ref, sc_ref, sh_ref, h_ref)

    h = h_ref[...]
    sub = w1_ref.shape[1] // MLP_SPLIT
    part = None
    for s in range(MLP_SPLIT):
        cs = slice(s * sub, (s + 1) * sub)
        hid = _dot(h, w1_ref[:, cs])
        hid = jnp.square(jnp.maximum(hid, 0.0)).astype(BF16)
        contrib = _dot(hid, w2_ref[cs, :])
        part = contrib if part is None else part + contrib

    @pl.when(f == 0)
    def _():
        o_ref[...] = part

    @pl.when(f > 0)
    def _():
        o_ref[...] += part

    @pl.when(f == pl.num_programs(1) - 1)
    def _():
        o_ref[...] = x_ref[...] + gate_ref[...] * o_ref[...]


def _mlp(rows, x, mods, norm_g, w1, w2, layer):
    assert w1.dtype == BF16 and w2.dtype == BF16
    d_ff = w1.shape[2]
    full = lambda j: 0
    return pl.pallas_call(
        _mlp_kernel,
        out_shape=jax.ShapeDtypeStruct((rows.m, D_MODEL), F32),
        grid=(rows.m // rows.tm, d_ff // TF_MLP),
        in_specs=[
            pl.BlockSpec((rows.tm, D_MODEL), lambda i, f: (i, 0)),
            pl.BlockSpec((1, D_MODEL), lambda i, f: (0, 0)),
            _mod_spec(rows, mods, 4, D_MODEL, full),
            _mod_spec(rows, mods, 3, D_MODEL, full),
            _mod_spec(rows, mods, 5, D_MODEL, full),
            pl.BlockSpec((None, D_MODEL, TF_MLP), lambda i, f: (layer, 0, f)),
            pl.BlockSpec((None, TF_MLP, D_MODEL), lambda i, f: (layer, f, 0)),
        ],
        out_specs=pl.BlockSpec((rows.tm, D_MODEL), lambda i, f: (i, 0)),
        scratch_shapes=[pltpu.VMEM((rows.tm, D_MODEL), BF16)],
        compiler_params=_cparams("parallel", "arbitrary"),
        name="sqrelu_mlp",
    )(x, norm_g.reshape(1, D_MODEL), mods, mods, mods, w1, w2)


def _proj_kernel(*refs, qk_norm, n_out):
    if qk_norm:
        x_ref, g_ref, sc_ref, sh_ref, w_ref, ng_ref, cos_ref, sa_ref, sb_ref = refs[:9]
        rest = refs[9:]
    else:
        x_ref, g_ref, sc_ref, sh_ref, w_ref = refs[:5]
        rest = refs[5:]
    out_refs, h_ref = rest[:n_out], rest[n_out]

    @pl.when(pl.program_id(1) == 0)
    def _():
        _norm_mod_to_bf16(x_ref, g_ref, sc_ref, sh_ref, h_ref)

    acc = _dot(h_ref[...], w_ref[...].astype(BF16))
    tn = acc.shape[1]
    if not qk_norm:
        for o_ref in out_refs:
            o_ref[...] = acc.astype(o_ref.dtype)
        return

    cos, sa, sb, ng = cos_ref[...], sa_ref[...], sb_ref[...], ng_ref[...]
    for grp in range(tn // HEAD_DIM):
        cols = slice(grp * HEAD_DIM, (grp + 1) * HEAD_DIM)
        a = acc[:, cols]
        y = a * lax.rsqrt(jnp.mean(a * a, axis=-1, keepdims=True) + EPS) * ng
        y = (y * cos + pltpu.roll(y, HEAD_DIM - ROT_DIM // 2, 1) * sa
             + pltpu.roll(y, ROT_DIM // 2, 1) * sb)
        for o_ref in out_refs:
            o_ref[:, cols] = y.astype(o_ref.dtype)


def _proj(rows, x, mods, k_shift, k_scale, norm_g, w, layer, out_dtypes, rope=None, head_g=None):
    n = w.shape[2]
    full = lambda j: 0
    qk_norm = rope is not None
    in_specs = [
        pl.BlockSpec((rows.tm, D_MODEL), lambda i, j: (i, 0)),
        pl.BlockSpec((1, D_MODEL), lambda i, j: (0, 0)),
        _mod_spec(rows, mods, k_scale, D_MODEL, full),
        _mod_spec(rows, mods, k_shift, D_MODEL, full),
        pl.BlockSpec((None, D_MODEL, TN_PROJ), lambda i, j: (layer, 0, j)),
    ]
    args = [x, norm_g.reshape(1, D_MODEL), mods, mods, w]
    if qk_norm:
        tps = rows.tiles_per_seq
        in_specs.append(pl.BlockSpec((1, HEAD_DIM), lambda i, j: (0, 0)))
        args.append(head_g.reshape(1, HEAD_DIM))
        for t in rope:
            in_specs.append(pl.BlockSpec((rows.tm, HEAD_DIM), lambda i, j: (i % tps, 0)))
            args.append(t)
    outs = pl.pallas_call(
        functools.partial(_proj_kernel, qk_norm=qk_norm, n_out=len(out_dtypes)),
        out_shape=[jax.ShapeDtypeStruct((rows.m, n), dt) for dt in out_dtypes],
        grid=(rows.m // rows.tm, n // TN_PROJ),
        in_specs=in_specs,
        out_specs=[pl.BlockSpec((rows.tm, TN_PROJ), lambda i, j: (i, j)) for _ in out_dtypes],
        scratch_shapes=[pltpu.VMEM((rows.tm, D_MODEL), BF16)],
        compiler_params=_cparams("parallel", "arbitrary"),
        name="norm_proj_rope" if qk_norm else "norm_proj",
    )(*args)
    return outs


def _res_linear_kernel(a_ref, w_ref, x_ref, gate_ref, o_ref):
    out = _dot(a_ref[...].astype(BF16), w_ref[...].astype(BF16))
    o_ref[...] = x_ref[...] + gate_ref[...] * out


def _res_linear(rows, a, w, layer, x, mods, k_gate):
    _, kdim, n = w.shape
    return pl.pallas_call(
        _res_linear_kernel,
        out_shape=jax.ShapeDtypeStruct((rows.m, n), F32),
        grid=(rows.m // rows.tm, n // TN_PROJ),
        in_specs=[
            pl.BlockSpec((rows.tm, kdim), lambda i, j: (i, 0)),
            pl.BlockSpec((None, kdim, TN_PROJ), lambda i, j: (layer, 0, j)),
            pl.BlockSpec((rows.tm, TN_PROJ), lambda i, j: (i, j)),
            _mod_spec(rows, mods, k_gate, TN_PROJ, lambda j: j),
        ],
        out_specs=pl.BlockSpec((rows.tm, TN_PROJ), lambda i, j: (i, j)),
        compiler_params=_cparams("parallel", "arbitrary"),
        name="out_proj_residual",
    )(a, w, x, mods)


def _lambda_full(lq1_ref, lk1_ref, lq2_ref, lk2_ref, lam_init):
    d1 = jnp.sum(lq1_ref[...] * lk1_ref[...], axis=-1, keepdims=True)
    d2 = jnp.sum(lq2_ref[...] * lk2_ref[...], axis=-1, keepdims=True)
    return jnp.exp(d1) - jnp.exp(d2) + lam_init


def _subln(o, g, lam_init):
    y = o * lax.rsqrt(jnp.mean(o * o, axis=-1, keepdims=True) + EPS)
    return y * g * (1.0 - lam_init)


def _attn_prompt_kernel(q_ref, k_ref, v_ref, lq1_ref, lk1_ref, lq2_ref, lk2_ref, sg_ref,
                        o_ref, m_ref, l_ref, acc_ref, *, lam_init):
    tq = q_ref.shape[0]
    tk, rs = TK_ATTN, ATTN_SUB_ROWS
    i = pl.program_id(2)
    m_ref[...] = jnp.full(m_ref.shape, NEG_BIG, F32)
    l_ref[...] = jnp.zeros(l_ref.shape, F32)
    acc_ref[...] = jnp.zeros(acc_ref.shape, F32)
    c2 = SCALE * math.log2(math.e)

    def chunk(c, diag):
        k0 = pl.multiple_of(c * tk, tk)
        vc = v_ref[pl.ds(k0, tk), :]
        for comp in range(2):
            cols = slice(comp * HEAD_DIM, (comp + 1) * HEAD_DIM)
            kc = k_ref[pl.ds(k0, tk), cols]
            for r in range(tq // rs):
                if diag is not None and (r + 1) * rs <= diag * tk:
                    continue
                rows = slice(r * rs, (r + 1) * rs)
                s = lax.dot_general(q_ref[rows, cols], kc, (((1,), (1,)), ((), ())),
                                    preferred_element_type=F32)
                if diag is not None:
                    row = lax.broadcasted_iota(jnp.int32, s.shape, 0) + r * rs
                    col = lax.broadcasted_iota(jnp.int32, s.shape, 1) + diag * tk
                    s = jnp.where(col <= row, s, NEG_BIG)
                tiles = [s[:, t * LANE:(t + 1) * LANE] for t in range(tk // LANE)]
                mx = functools.reduce(jnp.maximum, tiles)
                m_old = m_ref[comp, rows, :]
                m_new = jnp.maximum(m_old, jnp.max(mx, axis=-1, keepdims=True))
                alpha = jnp.exp2((m_old - m_new) * c2)
                ps = [jnp.exp2((t - m_new) * c2) for t in tiles]
                psum = functools.reduce(lambda a, b: a + b, ps)
                l_ref[comp, rows, :] = alpha * l_ref[comp, rows, :] + jnp.sum(psum, axis=-1, keepdims=True)
                m_ref[comp, rows, :] = m_new
                pv = _dot(jnp.concatenate([t.astype(BF16) for t in ps], axis=1), vc)
                for t in range(V_DIM // LANE):
                    lc = slice(t * LANE, (t + 1) * LANE)
                    acc_ref[comp, rows, lc] = alpha * acc_ref[comp, rows, lc] + pv[:, lc]

    def full_chunk(c, carry):
        chunk(c, None)
        return carry

    nd = tq // tk
    lax.fori_loop(0, i * nd, full_chunk, 0)
    for d in range(nd):
        chunk(i * nd + d, d)

    lam = _lambda_full(lq1_ref, lk1_ref, lq2_ref, lk2_ref, lam_init)
    outs = []
    for t in range(V_DIM // LANE):
        lc = slice(t * LANE, (t + 1) * LANE)
        outs.append(acc_ref[0, :, lc] / l_ref[0] - lam * (acc_ref[1, :, lc] / l_ref[1]))
    ms = sum(jnp.sum(o * o, axis=-1, keepdims=True) for o in outs) * (1.0 / V_DIM)
    inv = lax.rsqrt(ms + EPS) * (1.0 - lam_init)
    for t in range(V_DIM // LANE):
        lc = slice(t * LANE, (t + 1) * LANE)
        o_ref[:, lc] = (outs[t] * inv * sg_ref[:, lc]).astype(o_ref.dtype)


def _attn_prompt(q, k, v, lam_params, subln_g, lam_init, n_seq, seq):
    nq = seq // TQ_ATTN
    lam_spec = pl.BlockSpec((1, HEAD_DIM), lambda b, h, i: (0, 0))
    return pl.pallas_call(
        functools.partial(_attn_prompt_kernel, lam_init=lam_init),
        out_shape=jax.ShapeDtypeStruct(q.shape, BF16),
        grid=(n_seq, N_HEADS, nq),
        in_specs=[
            pl.BlockSpec((TQ_ATTN, V_DIM), lambda b, h, i: (b * nq + i, h)),
            pl.BlockSpec((seq, V_DIM), lambda b, h, i: (b, h)),
            pl.BlockSpec((seq, V_DIM), lambda b, h, i: (b, h)),
            lam_spec, lam_spec, lam_spec, lam_spec,
            pl.BlockSpec((1, V_DIM), lambda b, h, i: (0, 0)),
        ],
        out_specs=pl.BlockSpec((TQ_ATTN, V_DIM), lambda b, h, i: (b * nq + i, h)),
        scratch_shapes=[
            pltpu.VMEM((2, TQ_ATTN, LANE), F32),
            pltpu.VMEM((2, TQ_ATTN, LANE), F32),
            pltpu.VMEM((2, TQ_ATTN, V_DIM), F32),
        ],
        compiler_params=_cparams("parallel", "parallel", "arbitrary"),
        name="diff_attn_prompt",
    )(q, k, v, *[p.reshape(1, HEAD_DIM) for p in lam_params], subln_g.reshape(1, V_DIM))


def _attn_sample_kernel(pt_ref, q_ref, kn_ref, vn_ref, lq1_ref, lk1_ref, lq2_ref, lk2_ref, sg_ref,
                        *refs, lam_init):
    npg = PAGES_PER_STEP
    k_refs, v_refs = refs[:npg], refs[npg:2 * npg]
    o_ref, m_ref, l_ref, acc_ref = refs[2 * npg:]
    c = pl.program_id(1)
    nr = 2 * N_HEADS
    flat = k_refs[0].shape[0]

    @pl.when(c == 0)
    def _():
        m_ref[...] = jnp.full(m_ref.shape, NEG_BIG, F32)
        l_ref[...] = jnp.zeros(l_ref.shape, F32)
        acc_ref[...] = jnp.zeros(acc_ref.shape, F32)

    q = q_ref[...]
    q_bf = q.astype(BF16)
    row = lax.broadcasted_iota(jnp.int32, (nr, flat), 0)
    lane = lax.broadcasted_iota(jnp.int32, (nr, flat), 1)
    own = (N_HEADS - 1 - jnp.bitwise_and(row, N_HEADS - 1)) * 2 + jnp.right_shift(row, 3)
    valid = jnp.bitwise_and(lane, nr - 1) == own

    def align_to_values(pr):
        blocks = []
        for half in range(2):
            for comp in range(2):
                base = (half * N_HEADS - (N_HEADS - 1) - comp) % LANE
                tiles = [pltpu.roll(pr[comp * N_HEADS:(comp + 1) * N_HEADS, t * LANE:(t + 1) * LANE],
                                    base, 1, stride=1, stride_axis=0) for t in range(flat // LANE)]
                blocks.append(jnp.concatenate(tiles, axis=1))
        return jnp.concatenate(blocks, axis=0)

    m_old = m_ref[...]
    m_new = m_old
    s_pages = []
    for p in range(npg):
        s = lax.dot_general(q_bf, k_refs[p][...].astype(BF16), (((1,), (1,)), ((), ())),
                            preferred_element_type=F32) * SCALE
        s = jnp.where(valid, s, NEG_BIG)
        s_pages.append(s)
        m_new = jnp.maximum(m_new, jnp.max(s, axis=-1, keepdims=True))
    alpha = jnp.exp(m_old - m_new)
    l_new = alpha * l_ref[...]
    pv = jnp.zeros(acc_ref.shape, F32)
    for p in range(npg):
        pr = jnp.exp(s_pages[p] - m_new)
        l_new = l_new + jnp.sum(pr, axis=-1, keepdims=True)
        pv = pv + _dot(align_to_values(pr).astype(BF16), v_refs[p][...].astype(BF16))
    acc_ref[...] = jnp.concatenate([alpha, alpha], axis=0) * acc_ref[...] + pv
    l_ref[...] = l_new
    m_ref[...] = m_new

    @pl.when(c == pl.num_programs(1) - 1)
    def _():
        s_new = jnp.sum(q * kn_ref[...], axis=-1, keepdims=True) * SCALE
        m_fin = jnp.maximum(m_ref[...], s_new)
        a = jnp.exp(m_ref[...] - m_fin)
        p_new = jnp.exp(s_new - m_fin)
        l_fin = a * l_ref[...] + p_new
        two = lambda x: jnp.concatenate([x, x], axis=0)
        w_all = (two(a) * acc_ref[...] + two(p_new) * vn_ref[...]) / two(l_fin)
        lam = _lambda_full(lq1_ref, lk1_ref, lq2_ref, lk2_ref, lam_init)
        halves = [w_all[half * nr:half * nr + N_HEADS] - lam * w_all[half * nr + N_HEADS:(half + 1) * nr]
                  for half in range(2)]
        ms = sum(jnp.sum(d * d, axis=-1, keepdims=True) for d in halves) * (1.0 / V_DIM)
        inv = lax.rsqrt(ms + EPS) * (1.0 - lam_init)
        for half in range(2):
            o_ref[half * N_HEADS:(half + 1) * N_HEADS, :] = (
                halves[half] * inv * sg_ref[:, half * LANE:(half + 1) * LANE])


def _attn_sample(q, k_new, v_new, cache_k, cache_v, page_table, lam_params, subln_g, lam_init):
    n_seq, n_pages = page_table.shape
    n_pool = cache_k.shape[0]
    nr = 2 * N_HEADS
    flat = PAGE_SIZE * nr
    npg = PAGES_PER_STEP
    n_chunks = n_pages // npg

    k_flat = cache_k.reshape(n_pool, flat, HEAD_DIM)
    v_flat = cache_v.reshape(n_pool, PAGE_SIZE, N_HEADS, 2, LANE).transpose(0, 1, 3, 2, 4).reshape(n_pool, flat, LANE)

    def qk_rows(x):
        return x.reshape(n_seq, N_HEADS, 2, HEAD_DIM).transpose(0, 2, 1, 3)[:, :, ::-1].reshape(n_seq, nr, HEAD_DIM)

    v_rows = v_new.reshape(n_seq, N_HEADS, 2, LANE).transpose(0, 2, 1, 3)[:, :, ::-1]
    v_rows = jnp.broadcast_to(v_rows[:, :, None], (n_seq, 2, 2, N_HEADS, LANE)).reshape(n_seq, 2 * nr, LANE)

    row_spec = pl.BlockSpec((None, nr, HEAD_DIM), lambda b, c, pt: (b, 0, 0))
    lam_spec = pl.BlockSpec((1, HEAD_DIM), lambda b, c, pt: (0, 0))

    def page_spec(p):
        return pl.BlockSpec((None, flat, LANE), lambda b, c, pt: (pt[b * n_pages + c * npg + p], 0, 0))

    grid_spec = pltpu.PrefetchScalarGridSpec(
        num_scalar_prefetch=1,
        grid=(n_seq, n_chunks),
        in_specs=[row_spec, row_spec, pl.BlockSpec((None, 2 * nr, LANE), lambda b, c, pt: (b, 0, 0)),
                  lam_spec, lam_spec, lam_spec, lam_spec,
                  pl.BlockSpec((1, V_DIM), lambda b, c, pt: (0, 0))]
                 + [page_spec(p) for p in range(npg)] + [page_spec(p) for p in range(npg)],
        out_specs=row_spec,
        scratch_shapes=[
            pltpu.VMEM((nr, 1), F32),
            pltpu.VMEM((nr, 1), F32),
            pltpu.VMEM((2 * nr, LANE), F32),
        ],
    )
    out = pl.pallas_call(
        functools.partial(_attn_sample_kernel, lam_init=lam_init),
        out_shape=jax.ShapeDtypeStruct((n_seq, nr, LANE), F32),
        grid_spec=grid_spec,
        compiler_params=_cparams("parallel", "arbitrary"),
        name="diff_attn_sample",
    )(page_table.reshape(-1), qk_rows(q), qk_rows(k_new), v_rows,
      *[p.reshape(1, HEAD_DIM) for p in lam_params], subln_g.reshape(1, V_DIM),
      *([k_flat] * npg), *([v_flat] * npg))
    return out.reshape(n_seq, 2, N_HEADS, LANE)[:, :, ::-1].transpose(0, 2, 1, 3).reshape(n_seq, N_HEADS * V_DIM)


def _rope_tables(pos):
    half = ROT_DIM // 2
    inv_freq = jnp.power(ROPE_THETA, -jnp.arange(half, dtype=F32) * 2.0 / ROT_DIM)
    ang = pos.astype(F32)[:, None] * inv_freq[None, :]
    cos, sin = jnp.cos(ang), jnp.sin(ang)
    n = pos.shape[0]
    ones = jnp.ones((n, HEAD_DIM - ROT_DIM), F32)
    zeros_tail = jnp.zeros((n, HEAD_DIM - half), F32)
    cos_t = jnp.concatenate([cos, cos, ones], axis=1)
    sin_a = jnp.concatenate([-sin, zeros_tail], axis=1)
    sin_b = jnp.concatenate([jnp.zeros((n, half), F32), sin, jnp.zeros((n, HEAD_DIM - ROT_DIM), F32)], axis=1)
    return cos_t, sin_a, sin_b


def _lam_init(layer):
    return 0.8 - 0.6 * math.exp(-0.3 * layer)


def kernel(x_prompt, x_sample, state_conv, cache_k, cache_v, page_table, c_prompt, c_sample,
           ada_w, ada_b, norm_mix_g, norm_ff_g, w_ff1, w_ff2,
           conv_w_pw1, conv_b_pw1, conv_w_dw, conv_b_dw, conv_ln_g, conv_ln_b, conv_w_pw2, conv_b_pw2,
           kv_ada_w, kv_ada_b, kv_norm_g, w_k, w_v, k_norm_g,
           w_q, q_norm_g, lambda_q1, lambda_k1, lambda_q2, lambda_k2, subln_g, w_o):
    n_seq, seq, _ = x_prompt.shape
    n_dec = x_sample.shape[0]
    assert x_sample.shape[1] == 1 and n_dec <= SAMPLE_ROWS
    assert seq % TM_PROMPT == 0 and seq % TQ_ATTN == 0
    n_pages = page_table.shape[1]
    assert n_pages % PAGES_PER_STEP == 0
    n_past = n_pages * PAGE_SIZE
    width = N_HEADS * V_DIM

    rows_p = Rows(m=n_seq * seq, tm=TM_PROMPT, tiles_per_seq=seq // TM_PROMPT, per_row=False)
    rows_s = Rows(m=SAMPLE_ROWS, tm=SAMPLE_ROWS, tiles_per_seq=1, per_row=True)
    pad_s = SAMPLE_ROWS - n_dec

    c_all = jnp.concatenate([c_sample, c_prompt,
                             jnp.zeros((SAMPLE_ROWS - n_dec - n_seq, D_MODEL), F32)], axis=0)
    mods_all = _ada(c_all, ada_w, ada_b)
    kv_mods_all = _ada(c_all, kv_ada_w[None], kv_ada_b[None])[0]

    def group_mods(m, rows):
        if rows.per_row:
            return m
        return m[n_dec:n_dec + n_seq].reshape(n_seq, 1, m.shape[-1])

    w_ff1 = w_ff1.astype(BF16)
    w_ff2 = w_ff2.astype(BF16)

    def trunk(rows, x, rope, conv_fn, attn_fn):
        new_glu = []
        kv_mods = group_mods(kv_mods_all, rows)
        k_f32 = v_f32 = k_att = v_att = None
        for l in range(DEPTH):
            mods = group_mods(mods_all[l], rows)
            if l == N_A_LAYERS:
                k_f32, k_bf = _proj(rows, x, kv_mods, 0, 1, kv_norm_g, w_k[None], 0, (F32, BF16),
                                    rope=rope, head_g=k_norm_g)
                v_f32, v_bf = _proj(rows, x, kv_mods, 0, 1, kv_norm_g, w_v[None], 0, (F32, BF16))
                k_att, v_att = (k_f32, v_f32) if rows.per_row else (k_bf, v_bf)
            if l < N_A_LAYERS:
                glu = _pw1_glu(rows, x, mods, norm_mix_g[l], conv_w_pw1, l, conv_b_pw1[l])
                new_glu.append(glu)
                x = conv_fn(l, glu, x, mods)
            else:
                j = l - N_A_LAYERS
                q_dtype = F32 if rows.per_row else BF16
                (q,) = _proj(rows, x, mods, 0, 1, norm_mix_g[l], w_q, j, (q_dtype,),
                             rope=rope, head_g=q_norm_g[j])
                lam_params = (lambda_q1[j], lambda_k1[j], lambda_q2[j], lambda_k2[j])
                o = attn_fn(q, k_att, v_att, lam_params, subln_g[j], _lam_init(l))
                x = _res_linear(rows, o, w_o, j, x, mods, 2)
            x = _mlp(rows, x, mods, norm_ff_g[l], w_ff1, w_ff2, l)
        return x, new_glu, k_f32, v_f32

    rope_p = _rope_tables(jnp.arange(seq))
    state_p = jnp.zeros((n_seq, CONV_HALO, D_MODEL), F32)

    def conv_prompt(l, glu, x, mods):
        return _conv_pw2_prompt(rows_p, glu, state_p, x, mods, conv_w_dw[l], conv_b_dw[l],
                                conv_ln_g[l], conv_ln_b[l], conv_w_pw2, l, conv_b_pw2[l])

    def attn_prompt(q, k, v, lam_params, sg, lam_init):
        return _attn_prompt(q, k, v, lam_params, sg, lam_init, n_seq, seq)

    y_p, glu_p, k_p, v_p = trunk(rows_p, x_prompt.reshape(n_seq * seq, D_MODEL), rope_p,
                                 conv_prompt, attn_prompt)
    conv_p = jnp.stack([g.reshape(n_seq, seq, D_MODEL)[:, seq - (CONV_WIDTH - 1):] for g in glu_p])

    rope_s = _rope_tables(jnp.full((SAMPLE_ROWS,), n_past, jnp.int32))
    conv_s_rows = []

    def conv_sample(l, glu, x, mods):
        hist = jnp.pad(state_conv[l].transpose(1, 0, 2), ((0, 0), (0, pad_s), (0, 0)))
        full = jnp.concatenate([hist, glu[None]], axis=0)
        conv_s_rows.append(full[1:, :n_dec].transpose(1, 0, 2))
        return _conv_pw2_sample(rows_s, full, x, mods, conv_w_dw[l], conv_b_dw[l],
                                conv_ln_g[l], conv_ln_b[l], conv_w_pw2, l, conv_b_pw2[l])

    def attn_sample(q, k, v, lam_params, sg, lam_init):
        o = _attn_sample(q[:n_dec], k[:n_dec], v[:n_dec], cache_k, cache_v, page_table,
                         lam_params, sg, lam_init)
        return jnp.pad(o, ((0, pad_s), (0, 0)))

    x_s = jnp.pad(x_sample.reshape(n_dec, D_MODEL), ((0, pad_s), (0, 0)))
    y_s, _, k_s, v_s = trunk(rows_s, x_s, rope_s, conv_sample, attn_sample)

    return (y_p.reshape(n_seq, seq, D_MODEL),
            y_s[:n_dec].reshape(n_dec, 1, D_MODEL),
            conv_p,
            jnp.stack(conv_s_rows),
            k_p.reshape(n_seq, seq, N_HEADS, 2, HEAD_DIM),
            v_p.reshape(n_seq, seq, N_HEADS, V_DIM),
            k_s[:n_dec].reshape(n_dec, 1, N_HEADS, 2, HEAD_DIM),
            v_s[:n_dec].reshape(n_dec, 1, N_HEADS, V_DIM))
```

```python
import functools
import math
from typing import NamedTuple

import jax
import jax.numpy as jnp
from jax import lax
from jax.experimental import pallas as pl
from jax.experimental.pallas import tpu as pltpu

F32 = jnp.float32
BF16 = jnp.bfloat16

D_MODEL = 2048
DEPTH = 4
N_A_LAYERS = DEPTH // 2
CONV_WIDTH = 31
HEAD_DIM = 128
V_DIM = 2 * HEAD_DIM
N_HEADS = D_MODEL // V_DIM
ROT_DIM = HEAD_DIM // 4
ROPE_THETA = 500000.0
EPS = 1e-6
SCALE = HEAD_DIM ** -0.5
PAGE_SIZE = 128
NEG_BIG = -1e30

LANE = 128
SUBLANE = 8
VMEM_LIMIT_BYTES = 58 * 1024 * 1024

SAMPLE_ROWS = 16
TM_PROMPT = 512
TN_PROJ = 2048
TN_CONV = 1024
TN_GLU = 1024
TF_MLP = 2048
MLP_SPLIT = 4
TN_ADA = 1024
NORM_ROWS = 64
CONV_COLS = 256
CONV_ROWS = 64
CONV_HALO = 32
TQ_ATTN = 2048
TK_ATTN = 1024
ATTN_SUB_ROWS = 512
PAGES_PER_STEP = 8


class Rows(NamedTuple):
    m: int
    tm: int
    tiles_per_seq: int
    per_row: bool


def _cparams(*sem):
    return pltpu.CompilerParams(dimension_semantics=sem, vmem_limit_bytes=VMEM_LIMIT_BYTES)


def _mod_spec(rows, mods, k, width, col_of_j):
    nb = D_MODEL // width
    if rows.per_row:
        return pl.BlockSpec((rows.tm, width), lambda i, j: (0, k * nb + col_of_j(j)))
    return pl.BlockSpec((None, 1, width), lambda i, j: (i // rows.tiles_per_seq, 0, k * nb + col_of_j(j)))


def _rows_of(ref, r0, n):
    if ref.shape[0] == 1:
        return ref[...]
    return ref[pl.ds(r0, n), :]


def _norm_mod_to_bf16(x_ref, g_ref, sc_ref, sh_ref, h_ref):
    tm = x_ref.shape[0]
    rb = min(tm, NORM_ROWS)

    def body(r, carry):
        r0 = pl.multiple_of(r * rb, rb)
        x = x_ref[pl.ds(r0, rb), :]
        y = x * lax.rsqrt(jnp.mean(x * x, axis=-1, keepdims=True) + EPS)
        y = y * g_ref[...]
        h = y * (1.0 + _rows_of(sc_ref, r0, rb)) + _rows_of(sh_ref, r0, rb)
        h_ref[pl.ds(r0, rb), :] = h.astype(BF16)
        return carry

    lax.fori_loop(0, tm // rb, body, 0)


def _dot(a, b):
    return jnp.dot(a, b, preferred_element_type=F32)


def _ada_kernel(c_ref, w_ref, b_ref, o_ref):
    c = c_ref[...]
    s = (c * jax.nn.sigmoid(c)).astype(BF16)
    o_ref[...] = _dot(s, w_ref[...].astype(BF16)) + b_ref[...]


def _ada(c_all, w, b):
    n_layers, _, n = w.shape
    r = c_all.shape[0]
    return pl.pallas_call(
        _ada_kernel,
        out_shape=jax.ShapeDtypeStruct((n_layers, r, n), F32),
        grid=(n_layers, n // TN_ADA),
        in_specs=[
            pl.BlockSpec((r, D_MODEL), lambda l, j: (0, 0)),
            pl.BlockSpec((None, D_MODEL, TN_ADA), lambda l, j: (l, 0, j)),
            pl.BlockSpec((None, 1, TN_ADA), lambda l, j: (l, 0, j)),
        ],
        out_specs=pl.BlockSpec((None, r, TN_ADA), lambda l, j: (l, 0, j)),
        compiler_params=_cparams("parallel", "parallel"),
        name="ada_mod",
    )(c_all, w, b.reshape(n_layers, 1, n))


def _pw1_glu_kernel(x_ref, g_ref, sc_ref, sh_ref, wa_ref, wg_ref, ba_ref, bg_ref, o_ref, h_ref):
    @pl.when(pl.program_id(1) == 0)
    def _():
        _norm_mod_to_bf16(x_ref, g_ref, sc_ref, sh_ref, h_ref)

    h = h_ref[...]
    a = _dot(h, wa_ref[...]) + ba_ref[...]
    gt = _dot(h, wg_ref[...]) + bg_ref[...]
    o_ref[...] = a * jax.nn.sigmoid(gt)


def _pw1_glu(rows, x, mods, norm_g, w, layer, b):
    nj = D_MODEL // TN_GLU
    b2 = b.reshape(1, 2 * D_MODEL)
    full = lambda j: 0
    return pl.pallas_call(
        _pw1_glu_kernel,
        out_shape=jax.ShapeDtypeStruct((rows.m, D_MODEL), F32),
        grid=(rows.m // rows.tm, nj),
        in_specs=[
            pl.BlockSpec((rows.tm, D_MODEL), lambda i, j: (i, 0)),
            pl.BlockSpec((1, D_MODEL), lambda i, j: (0, 0)),
            _mod_spec(rows, mods, 1, D_MODEL, full),
            _mod_spec(rows, mods, 0, D_MODEL, full),
            pl.BlockSpec((None, D_MODEL, TN_GLU), lambda i, j: (layer, 0, j)),
            pl.BlockSpec((None, D_MODEL, TN_GLU), lambda i, j: (layer, 0, j + nj)),
            pl.BlockSpec((1, TN_GLU), lambda i, j: (0, j)),
            pl.BlockSpec((1, TN_GLU), lambda i, j: (0, j + nj)),
        ],
        out_specs=pl.BlockSpec((rows.tm, TN_GLU), lambda i, j: (i, j)),
        scratch_shapes=[pltpu.VMEM((rows.tm, D_MODEL), BF16)],
        compiler_params=_cparams("parallel", "arbitrary"),
        name="pw1_glu",
    )(x, norm_g.reshape(1, D_MODEL), mods, mods, w, w, b2, b2)


def _ln_silu_to_bf16(yc_ref, lng_ref, lnb_ref, y_ref):
    ncb, tm, cw = yc_ref.shape
    rb = min(tm, NORM_ROWS)

    def body(r, carry):
        r0 = pl.multiple_of(r * rb, rb)
        parts = [yc_ref[cb, pl.ds(r0, rb), :] for cb in range(ncb)]
        mu = sum(jnp.sum(p, axis=-1, keepdims=True) for p in parts) * (1.0 / D_MODEL)
        var = sum(jnp.sum(jnp.square(p - mu), axis=-1, keepdims=True) for p in parts) * (1.0 / D_MODEL)
        inv = lax.rsqrt(var + EPS)
        for cb in range(ncb):
            cols = slice(cb * cw, (cb + 1) * cw)
            y = (parts[cb] - mu) * inv * lng_ref[:, cols] + lnb_ref[:, cols]
            y = y * jax.nn.sigmoid(y)
            y_ref[pl.ds(r0, rb), cols] = y.astype(BF16)
        return carry

    lax.fori_loop(0, tm // rb, body, 0)


def _pw2_residual(y_ref, w_ref, b_ref, x_ref, gate_ref, o_ref):
    out = _dot(y_ref[...], w_ref[...]) + b_ref[...]
    o_ref[...] = x_ref[...] + gate_ref[...] * out


def _conv_prompt_kernel(glu_ref, halo_ref, st_ref, wd_ref, bd_ref, lng_ref, lnb_ref,
                        w_ref, b_ref, x_ref, gate_ref, o_ref, win_ref, shf_ref, yc_ref, y_ref, *, tiles_per_seq):
    ncb, win_rows, cw = win_ref.shape
    tm = glu_ref.shape[0]

    @pl.when(pl.program_id(1) == 0)
    def _():
        first = pl.program_id(0) % tiles_per_seq == 0
        for cb in range(ncb):
            cols = slice(cb * cw, (cb + 1) * cw)
            win_ref[cb, 0:CONV_HALO, :] = jnp.where(first, st_ref[:, cols], halo_ref[:, cols])
            win_ref[cb, CONV_HALO:, :] = glu_ref[:, cols]

        base = CONV_HALO - (CONV_WIDTH - 1)

        def conv_cols(cb, carry):
            wd = wd_ref[cb]
            for r in range(1, SUBLANE):
                shf_ref[r - 1, 0:win_rows - SUBLANE, :] = win_ref[cb, r:r + win_rows - SUBLANE, :]
            for rb in range(tm // CONV_ROWS):
                acc = jnp.zeros((CONV_ROWS, cw), F32)
                for w in range(CONV_WIDTH):
                    r = (base + w) % SUBLANE
                    start = rb * CONV_ROWS + (base + w) - r
                    if r == 0:
                        tap = win_ref[cb, pl.ds(start, CONV_ROWS), :]
                    else:
                        tap = shf_ref[r - 1, pl.ds(start, CONV_ROWS), :]
                    acc = acc + tap * wd[w:w + 1, :]
                yc_ref[cb, pl.ds(rb * CONV_ROWS, CONV_ROWS), :] = acc + bd_ref[cb]
            return carry

        lax.fori_loop(0, ncb, conv_cols, 0)
        _ln_silu_to_bf16(yc_ref, lng_ref, lnb_ref, y_ref)

    _pw2_residual(y_ref, w_ref, b_ref, x_ref, gate_ref, o_ref)


def _conv_sample_kernel(full_ref, wd_ref, bd_ref, lng_ref, lnb_ref,
                        w_ref, b_ref, x_ref, gate_ref, o_ref, yc_ref, y_ref):
    ncb, _, cw = yc_ref.shape

    @pl.when(pl.program_id(1) == 0)
    def _():
        for cb in range(ncb):
            cols = slice(cb * cw, (cb + 1) * cw)
            wd = wd_ref[cb]
            acc = jnp.zeros((full_ref.shape[1], cw), F32)
            for w in range(CONV_WIDTH):
                acc = acc + full_ref[w, :, cols] * wd[w:w + 1, :]
            yc_ref[cb] = acc + bd_ref[cb]
        _ln_silu_to_bf16(yc_ref, lng_ref, lnb_ref, y_ref)

    _pw2_residual(y_ref, w_ref, b_ref, x_ref, gate_ref, o_ref)


def _conv_weights(w_dw, b_dw):
    ncb = D_MODEL // CONV_COLS
    wd = jnp.pad(w_dw, ((0, CONV_HALO - CONV_WIDTH), (0, 0)))
    wd = wd.reshape(CONV_HALO, ncb, CONV_COLS).transpose(1, 0, 2)
    return wd, b_dw.reshape(ncb, 1, CONV_COLS)


def _conv_pw2_prompt(rows, glu, state, x, mods, w_dw, b_dw, ln_g, ln_b, w, layer, b):
    ncb = D_MODEL // CONV_COLS
    nj = D_MODEL // TN_CONV
    wd, bd = _conv_weights(w_dw, b_dw)
    halo_per_tile = rows.tm // CONV_HALO
    tps = rows.tiles_per_seq
    const2 = lambda i, j: (0, 0)
    const3 = lambda i, j: (0, 0, 0)
    return pl.pallas_call(
        functools.partial(_conv_prompt_kernel, tiles_per_seq=tps),
        out_shape=jax.ShapeDtypeStruct((rows.m, D_MODEL), F32),
        grid=(rows.m // rows.tm, nj),
        in_specs=[
            pl.BlockSpec((rows.tm, D_MODEL), lambda i, j: (i, 0)),
            pl.BlockSpec((CONV_HALO, D_MODEL), lambda i, j: (jnp.maximum(i * halo_per_tile - 1, 0), 0)),
            pl.BlockSpec((None, CONV_HALO, D_MODEL), lambda i, j: (i // tps, 0, 0)),
            pl.BlockSpec((ncb, CONV_HALO, CONV_COLS), const3),
            pl.BlockSpec((ncb, 1, CONV_COLS), const3),
            pl.BlockSpec((1, D_MODEL), const2),
            pl.BlockSpec((1, D_MODEL), const2),
            pl.BlockSpec((None, D_MODEL, TN_CONV), lambda i, j: (layer, 0, j)),
            pl.BlockSpec((1, TN_CONV), lambda i, j: (0, j)),
            pl.BlockSpec((rows.tm, TN_CONV), lambda i, j: (i, j)),
            _mod_spec(rows, mods, 2, TN_CONV, lambda j: j),
        ],
        out_specs=pl.BlockSpec((rows.tm, TN_CONV), lambda i, j: (i, j)),
        scratch_shapes=[
            pltpu.VMEM((ncb, CONV_HALO + rows.tm, CONV_COLS), F32),
            pltpu.VMEM((SUBLANE - 1, CONV_HALO + rows.tm, CONV_COLS), F32),
            pltpu.VMEM((ncb, rows.tm, CONV_COLS), F32),
            pltpu.VMEM((rows.tm, D_MODEL), BF16),
        ],
        compiler_params=_cparams("parallel", "arbitrary"),
        name="conv_pw2_prompt",
    )(glu, glu, state, wd, bd, ln_g.reshape(1, D_MODEL), ln_b.reshape(1, D_MODEL),
      w, b.reshape(1, D_MODEL), x, mods)


def _conv_pw2_sample(rows, full, x, mods, w_dw, b_dw, ln_g, ln_b, w, layer, b):
    ncb = D_MODEL // CONV_COLS
    nj = D_MODEL // TN_CONV
    wd, bd = _conv_weights(w_dw, b_dw)
    const2 = lambda i, j: (0, 0)
    const3 = lambda i, j: (0, 0, 0)
    return pl.pallas_call(
        _conv_sample_kernel,
        out_shape=jax.ShapeDtypeStruct((rows.m, D_MODEL), F32),
        grid=(rows.m // rows.tm, nj),
        in_specs=[
            pl.BlockSpec((CONV_WIDTH, rows.tm, D_MODEL), lambda i, j: (0, i, 0)),
            pl.BlockSpec((ncb, CONV_HALO, CONV_COLS), const3),
            pl.BlockSpec((ncb, 1, CONV_COLS), const3),
            pl.BlockSpec((1, D_MODEL), const2),
            pl.BlockSpec((1, D_MODEL), const2),
            pl.BlockSpec((None, D_MODEL, TN_CONV), lambda i, j: (layer, 0, j)),
            pl.BlockSpec((1, TN_CONV), lambda i, j: (0, j)),
            pl.BlockSpec((rows.tm, TN_CONV), lambda i, j: (i, j)),
            _mod_spec(rows, mods, 2, TN_CONV, lambda j: j),
        ],
        out_specs=pl.BlockSpec((rows.tm, TN_CONV), lambda i, j: (i, j)),
        scratch_shapes=[
            pltpu.VMEM((ncb, rows.tm, CONV_COLS), F32),
            pltpu.VMEM((rows.tm, D_MODEL), BF16),
        ],
        compiler_params=_cparams("parallel", "arbitrary"),
        name="conv_pw2_sample",
    )(full, wd, bd, ln_g.reshape(1, D_MODEL), ln_b.reshape(1, D_MODEL),
      w, b.reshape(1, D_MODEL), x, mods)


def _mlp_kernel(x_ref, g_ref, sc_ref, sh_ref, gate_ref, w1_ref, w2_ref, o_ref, h_ref):
    f = pl.program_id(1)

    @pl.when(f == 0)
    def _():
        _norm_mod_to_bf16(x_ref, g_ref, sc_ref, sh_ref, h_ref)

    h = h_ref[...]
    sub = w1_ref.shape[1] // MLP_SPLIT
    part = None
    for s in range(MLP_SPLIT):
        cs = slice(s * sub, (s + 1) * sub)
        hid = _dot(h, w1_ref[:, cs])
        hid = jnp.square(jnp.maximum(hid, 0.0)).astype(BF16)
        contrib = _dot(hid, w2_ref[cs, :])
        part = contrib if part is None else part + contrib

    @pl.when(f == 0)
    def _():
        o_ref[...] = part

    @pl.when(f > 0)
    def _():
        o_ref[...] += part

    @pl.when(f == pl.num_programs(1) - 1)
    def _():
        o_ref[...] = x_ref[...] + gate_ref[...] * o_ref[...]


def _mlp(rows, x, mods, norm_g, w1, w2, layer):
    assert w1.dtype == BF16 and w2.dtype == BF16
    d_ff = w1.shape[2]
    full = lambda j: 0
    return pl.pallas_call(
        _mlp_kernel,
        out_shape=jax.ShapeDtypeStruct((rows.m, D_MODEL), F32),
        grid=(rows.m // rows.tm, d_ff // TF_MLP),
        in_specs=[
            pl.BlockSpec((rows.tm, D_MODEL), lambda i, f: (i, 0)),
            pl.BlockSpec((1, D_MODEL), lambda i, f: (0, 0)),
            _mod_spec(rows, mods, 4, D_MODEL, full),
            _mod_spec(rows, mods, 3, D_MODEL, full),
            _mod_spec(rows, mods, 5, D_MODEL, full),
            pl.BlockSpec((None, D_MODEL, TF_MLP), lambda i, f: (layer, 0, f)),
            pl.BlockSpec((None, TF_MLP, D_MODEL), lambda i, f: (layer, f, 0)),
        ],
        out_specs=pl.BlockSpec((rows.tm, D_MODEL), lambda i, f: (i, 0)),
        scratch_shapes=[pltpu.VMEM((rows.tm, D_MODEL), BF16)],
        compiler_params=_cparams("parallel", "arbitrary"),
        name="sqrelu_mlp",
    )(x, norm_g.reshape(1, D_MODEL), mods, mods, mods, w1, w2)


def _proj_kernel(*refs, qk_norm, n_out):
    if qk_norm:
        x_ref, g_ref, sc_ref, sh_ref, w_ref, ng_ref, cos_ref, sa_ref, sb_ref = refs[:9]
        rest = refs[9:]
    else:
        x_ref, g_ref, sc_ref, sh_ref, w_ref = refs[:5]
        rest = refs[5:]
    out_refs, h_ref = rest[:n_out], rest[n_out]

    @pl.when(pl.program_id(1) == 0)
    def _():
        _norm_mod_to_bf16(x_ref, g_ref, sc_ref, sh_ref, h_ref)

    acc = _dot(h_ref[...], w_ref[...])
    tn = acc.shape[1]
    if not qk_norm:
        for o_ref in out_refs:
            o_ref[...] = acc.astype(o_ref.dtype)
        return

    cos, sa, sb, ng = cos_ref[...], sa_ref[...], sb_ref[...], ng_ref[...]
    for grp in range(tn // HEAD_DIM):
        cols = slice(grp * HEAD_DIM, (grp + 1) * HEAD_DIM)
        a = acc[:, cols]
        y = a * lax.rsqrt(jnp.mean(a * a, axis=-1, keepdims=True) + EPS) * ng
        y = (y * cos + pltpu.roll(y, HEAD_DIM - ROT_DIM // 2, 1) * sa
             + pltpu.roll(y, ROT_DIM // 2, 1) * sb)
        for o_ref in out_refs:
            o_ref[:, cols] = y.astype(o_ref.dtype)


def _proj(rows, x, mods, k_shift, k_scale, norm_g, w, layer, out_dtypes, rope=None, head_g=None):
    n = w.shape[2]
    full = lambda j: 0
    qk_norm = rope is not None
    in_specs = [
        pl.BlockSpec((rows.tm, D_MODEL), lambda i, j: (i, 0)),
        pl.BlockSpec((1, D_MODEL), lambda i, j: (0, 0)),
        _mod_spec(rows, mods, k_scale, D_MODEL, full),
        _mod_spec(rows, mods, k_shift, D_MODEL, full),
        pl.BlockSpec((None, D_MODEL, TN_PROJ), lambda i, j: (layer, 0, j)),
    ]
    args = [x, norm_g.reshape(1, D_MODEL), mods, mods, w]
    if qk_norm:
        tps = rows.tiles_per_seq
        in_specs.append(pl.BlockSpec((1, HEAD_DIM), lambda i, j: (0, 0)))
        args.append(head_g.reshape(1, HEAD_DIM))
        for t in rope:
            in_specs.append(pl.BlockSpec((rows.tm, HEAD_DIM), lambda i, j: (i % tps, 0)))
            args.append(t)
    outs = pl.pallas_call(
        functools.partial(_proj_kernel, qk_norm=qk_norm, n_out=len(out_dtypes)),
        out_shape=[jax.ShapeDtypeStruct((rows.m, n), dt) for dt in out_dtypes],
        grid=(rows.m // rows.tm, n // TN_PROJ),
        in_specs=in_specs,
        out_specs=[pl.BlockSpec((rows.tm, TN_PROJ), lambda i, j: (i, j)) for _ in out_dtypes],
        scratch_shapes=[pltpu.VMEM((rows.tm, D_MODEL), BF16)],
        compiler_params=_cparams("parallel", "arbitrary"),
        name="norm_proj_rope" if qk_norm else "norm_proj",
    )(*args)
    return outs


def _res_linear_kernel(a_ref, w_ref, x_ref, gate_ref, o_ref):
    out = _dot(a_ref[...].astype(BF16), w_ref[...])
    o_ref[...] = x_ref[...] + gate_ref[...] * out


def _res_linear(rows, a, w, layer, x, mods, k_gate):
    _, kdim, n = w.shape
    return pl.pallas_call(
        _res_linear_kernel,
        out_shape=jax.ShapeDtypeStruct((rows.m, n), F32),
        grid=(rows.m // rows.tm, n // TN_PROJ),
        in_specs=[
            pl.BlockSpec((rows.tm, kdim), lambda i, j: (i, 0)),
            pl.BlockSpec((None, kdim, TN_PROJ), lambda i, j: (layer, 0, j)),
            pl.BlockSpec((rows.tm, TN_PROJ), lambda i, j: (i, j)),
            _mod_spec(rows, mods, k_gate, TN_PROJ, lambda j: j),
        ],
        out_specs=pl.BlockSpec((rows.tm, TN_PROJ), lambda i, j: (i, j)),
        compiler_params=_cparams("parallel", "arbitrary"),
        name="out_proj_residual",
    )(a, w, x, mods)


def _lambda_full(lq1_ref, lk1_ref, lq2_ref, lk2_ref, lam_init):
    d1 = jnp.sum(lq1_ref[...] * lk1_ref[...], axis=-1, keepdims=True)
    d2 = jnp.sum(lq2_ref[...] * lk2_ref[...], axis=-1, keepdims=True)
    return jnp.exp(d1) - jnp.exp(d2) + lam_init


def _subln(o, g, lam_init):
    y = o * lax.rsqrt(jnp.mean(o * o, axis=-1, keepdims=True) + EPS)
    return y * g * (1.0 - lam_init)


def _attn_prompt_kernel(q_ref, k_ref, v_ref, lq1_ref, lk1_ref, lq2_ref, lk2_ref, sg_ref,
                        o_ref, m_ref, l_ref, acc_ref, *, lam_init):
    tq = q_ref.shape[0]
    tk, rs = TK_ATTN, ATTN_SUB_ROWS
    i = pl.program_id(2)
    m_ref[...] = jnp.full(m_ref.shape, NEG_BIG, F32)
    l_ref[...] = jnp.zeros(l_ref.shape, F32)
    acc_ref[...] = jnp.zeros(acc_ref.shape, F32)
    c2 = SCALE * math.log2(math.e)

    def chunk(c, diag):
        k0 = pl.multiple_of(c * tk, tk)
        vc = v_ref[pl.ds(k0, tk), :]
        for comp in range(2):
            cols = slice(comp * HEAD_DIM, (comp + 1) * HEAD_DIM)
            kc = k_ref[pl.ds(k0, tk), cols]
            for r in range(tq // rs):
                if diag is not None and (r + 1) * rs <= diag * tk:
                    continue
                rows = slice(r * rs, (r + 1) * rs)
                s = lax.dot_general(q_ref[rows, cols], kc, (((1,), (1,)), ((), ())),
                                    preferred_element_type=F32)
                if diag is not None:
                    row = lax.broadcasted_iota(jnp.int32, s.shape, 0) + r * rs
                    col = lax.broadcasted_iota(jnp.int32, s.shape, 1) + diag * tk
                    s = jnp.where(col <= row, s, NEG_BIG)
                tiles = [s[:, t * LANE:(t + 1) * LANE] for t in range(tk // LANE)]
                mx = functools.reduce(jnp.maximum, tiles)
                m_old = m_ref[comp, rows, :]
                m_new = jnp.maximum(m_old, jnp.max(mx, axis=-1, keepdims=True))
                alpha = jnp.exp2((m_old - m_new) * c2)
                ps = [jnp.exp2((t - m_new) * c2) for t in tiles]
                psum = functools.reduce(lambda a, b: a + b, ps)
                l_ref[comp, rows, :] = alpha * l_ref[comp, rows, :] + jnp.sum(psum, axis=-1, keepdims=True)
                m_ref[comp, rows, :] = m_new
                pv = _dot(jnp.concatenate([t.astype(BF16) for t in ps], axis=1), vc)
                for t in range(V_DIM // LANE):
                    lc = slice(t * LANE, (t + 1) * LANE)
                    acc_ref[comp, rows, lc] = alpha * acc_ref[comp, rows, lc] + pv[:, lc]

    def full_chunk(c, carry):
        chunk(c, None)
        return carry

    nd = tq // tk
    lax.fori_loop(0, i * nd, full_chunk, 0)
    for d in range(nd):
        chunk(i * nd + d, d)

    lam = _lambda_full(lq1_ref, lk1_ref, lq2_ref, lk2_ref, lam_init)
    outs = []
    for t in range(V_DIM // LANE):
        lc = slice(t * LANE, (t + 1) * LANE)
        outs.append(acc_ref[0, :, lc] / l_ref[0] - lam * (acc_ref[1, :, lc] / l_ref[1]))
    ms = sum(jnp.sum(o * o, axis=-1, keepdims=True) for o in outs) * (1.0 / V_DIM)
    inv = lax.rsqrt(ms + EPS) * (1.0 - lam_init)
    for t in range(V_DIM // LANE):
        lc = slice(t * LANE, (t + 1) * LANE)
        o_ref[:, lc] = (outs[t] * inv * sg_ref[:, lc]).astype(o_ref.dtype)


def _attn_prompt(q, k, v, lam_params, subln_g, lam_init, n_seq, seq):
    nq = seq // TQ_ATTN
    lam_spec = pl.BlockSpec((1, HEAD_DIM), lambda b, h, i: (0, 0))
    return pl.pallas_call(
        functools.partial(_attn_prompt_kernel, lam_init=lam_init),
        out_shape=jax.ShapeDtypeStruct(q.shape, BF16),
        grid=(n_seq, N_HEADS, nq),
        in_specs=[
            pl.BlockSpec((TQ_ATTN, V_DIM), lambda b, h, i: (b * nq + i, h)),
            pl.BlockSpec((seq, V_DIM), lambda b, h, i: (b, h)),
            pl.BlockSpec((seq, V_DIM), lambda b, h, i: (b, h)),
            lam_spec, lam_spec, lam_spec, lam_spec,
            pl.BlockSpec((1, V_DIM), lambda b, h, i: (0, 0)),
        ],
        out_specs=pl.BlockSpec((TQ_ATTN, V_DIM), lambda b, h, i: (b * nq + i, h)),
        scratch_shapes=[
            pltpu.VMEM((2, TQ_ATTN, LANE), F32),
            pltpu.VMEM((2, TQ_ATTN, LANE), F32),
            pltpu.VMEM((2, TQ_ATTN, V_DIM), F32),
        ],
        compiler_params=_cparams("parallel", "parallel", "arbitrary"),
        name="diff_attn_prompt",
    )(q, k, v, *[p.reshape(1, HEAD_DIM) for p in lam_params], subln_g.reshape(1, V_DIM))


def _attn_sample_kernel(pt_ref, q_ref, kn_ref, vn_ref, lq1_ref, lk1_ref, lq2_ref, lk2_ref, sg_ref,
                        *refs, lam_init):
    npg = PAGES_PER_STEP
    k_refs, v_refs = refs[:npg], refs[npg:2 * npg]
    o_ref, m_ref, l_ref, acc_ref = refs[2 * npg:]
    c = pl.program_id(1)
    nr = 2 * N_HEADS
    flat = k_refs[0].shape[0]

    @pl.when(c == 0)
    def _():
        m_ref[...] = jnp.full(m_ref.shape, NEG_BIG, F32)
        l_ref[...] = jnp.zeros(l_ref.shape, F32)
        acc_ref[...] = jnp.zeros(acc_ref.shape, F32)

    q = q_ref[...]
    q_bf = q.astype(BF16)
    row = lax.broadcasted_iota(jnp.int32, (nr, flat), 0)
    lane = lax.broadcasted_iota(jnp.int32, (nr, flat), 1)
    own = (N_HEADS - 1 - jnp.bitwise_and(row, N_HEADS - 1)) * 2 + jnp.right_shift(row, 3)
    valid = jnp.bitwise_and(lane, nr - 1) == own

    def align_to_values(pr):
        blocks = []
        for half in range(2):
            for comp in range(2):
                base = (half * N_HEADS - (N_HEADS - 1) - comp) % LANE
                tiles = [pltpu.roll(pr[comp * N_HEADS:(comp + 1) * N_HEADS, t * LANE:(t + 1) * LANE],
                                    base, 1, stride=1, stride_axis=0) for t in range(flat // LANE)]
                blocks.append(jnp.concatenate(tiles, axis=1))
        return jnp.concatenate(blocks, axis=0)

    m_old = m_ref[...]
    m_new = m_old
    s_pages = []
    for p in range(npg):
        s = lax.dot_general(q_bf, k_refs[p][...].astype(BF16), (((1,), (1,)), ((), ())),
                            preferred_element_type=F32) * SCALE
        s = jnp.where(valid, s, NEG_BIG)
        s_pages.append(s)
        m_new = jnp.maximum(m_new, jnp.max(s, axis=-1, keepdims=True))
    alpha = jnp.exp(m_old - m_new)
    l_new = alpha * l_ref[...]
    pv = jnp.zeros(acc_ref.shape, F32)
    for p in range(npg):
        pr = jnp.exp(s_pages[p] - m_new)
        l_new = l_new + jnp.sum(pr, axis=-1, keepdims=True)
        pv = pv + _dot(align_to_values(pr).astype(BF16), v_refs[p][...].astype(BF16))
    acc_ref[...] = jnp.concatenate([alpha, alpha], axis=0) * acc_ref[...] + pv
    l_ref[...] = l_new
    m_ref[...] = m_new

    @pl.when(c == pl.num_programs(1) - 1)
    def _():
        s_new = jnp.sum(q * kn_ref[...], axis=-1, keepdims=True) * SCALE
        m_fin = jnp.maximum(m_ref[...], s_new)
        a = jnp.exp(m_ref[...] - m_fin)
        p_new = jnp.exp(s_new - m_fin)
        l_fin = a * l_ref[...] + p_new
        two = lambda x: jnp.concatenate([x, x], axis=0)
        w_all = (two(a) * acc_ref[...] + two(p_new) * vn_ref[...]) / two(l_fin)
        lam = _lambda_full(lq1_ref, lk1_ref, lq2_ref, lk2_ref, lam_init)
        halves = [w_all[half * nr:half * nr + N_HEADS] - lam * w_all[half * nr + N_HEADS:(half + 1) * nr]
                  for half in range(2)]
        ms = sum(jnp.sum(d * d, axis=-1, keepdims=True) for d in halves) * (1.0 / V_DIM)
        inv = lax.rsqrt(ms + EPS) * (1.0 - lam_init)
        for half in range(2):
            o_ref[half * N_HEADS:(half + 1) * N_HEADS, :] = (
                halves[half] * inv * sg_ref[:, half * LANE:(half + 1) * LANE])


def _attn_sample(q, k_new, v_new, cache_k, cache_v, page_table, lam_params, subln_g, lam_init):
    n_seq, n_pages = page_table.shape
    n_pool = cache_k.shape[0]
    nr = 2 * N_HEADS
    flat = PAGE_SIZE * nr
    npg = PAGES_PER_STEP
    n_chunks = n_pages // npg

    k_flat = cache_k.reshape(n_pool, flat, HEAD_DIM)
    v_flat = cache_v.reshape(n_pool, PAGE_SIZE, N_HEADS, 2, LANE).transpose(0, 1, 3, 2, 4).reshape(n_pool, flat, LANE)

    def qk_rows(x):
        return x.reshape(n_seq, N_HEADS, 2, HEAD_DIM).transpose(0, 2, 1, 3)[:, :, ::-1].reshape(n_seq, nr, HEAD_DIM)

    v_rows = v_new.reshape(n_seq, N_HEADS, 2, LANE).transpose(0, 2, 1, 3)[:, :, ::-1]
    v_rows = jnp.broadcast_to(v_rows[:, :, None], (n_seq, 2, 2, N_HEADS, LANE)).reshape(n_seq, 2 * nr, LANE)

    row_spec = pl.BlockSpec((None, nr, HEAD_DIM), lambda b, c, pt: (b, 0, 0))
    lam_spec = pl.BlockSpec((1, HEAD_DIM), lambda b, c, pt: (0, 0))

    def page_spec(p):
        return pl.BlockSpec((None, flat, LANE), lambda b, c, pt: (pt[b * n_pages + c * npg + p], 0, 0))

    grid_spec = pltpu.PrefetchScalarGridSpec(
        num_scalar_prefetch=1,
        grid=(n_seq, n_chunks),
        in_specs=[row_spec, row_spec, pl.BlockSpec((None, 2 * nr, LANE), lambda b, c, pt: (b, 0, 0)),
                  lam_spec, lam_spec, lam_spec, lam_spec,
                  pl.BlockSpec((1, V_DIM), lambda b, c, pt: (0, 0))]
                 + [page_spec(p) for p in range(npg)] + [page_spec(p) for p in range(npg)],
        out_specs=row_spec,
        scratch_shapes=[
            pltpu.VMEM((nr, 1), F32),
            pltpu.VMEM((nr, 1), F32),
            pltpu.VMEM((2 * nr, LANE), F32),
        ],
    )
    out = pl.pallas_call(
        functools.partial(_attn_sample_kernel, lam_init=lam_init),
        out_shape=jax.ShapeDtypeStruct((n_seq, nr, LANE), F32),
        grid_spec=grid_spec,
        compiler_params=_cparams("parallel", "arbitrary"),
        name="diff_attn_sample",
    )(page_table.reshape(-1), qk_rows(q), qk_rows(k_new), v_rows,
      *[p.reshape(1, HEAD_DIM) for p in lam_params], subln_g.reshape(1, V_DIM),
      *([k_flat] * npg), *([v_flat] * npg))
    return out.reshape(n_seq, 2, N_HEADS, LANE)[:, :, ::-1].transpose(0, 2, 1, 3).reshape(n_seq, N_HEADS * V_DIM)


def _rope_tables(pos):
    half = ROT_DIM // 2
    inv_freq = jnp.power(ROPE_THETA, -jnp.arange(half, dtype=F32) * 2.0 / ROT_DIM)
    ang = pos.astype(F32)[:, None] * inv_freq[None, :]
    cos, sin = jnp.cos(ang), jnp.sin(ang)
    n = pos.shape[0]
    ones = jnp.ones((n, HEAD_DIM - ROT_DIM), F32)
    zeros_tail = jnp.zeros((n, HEAD_DIM - half), F32)
    cos_t = jnp.concatenate([cos, cos, ones], axis=1)
    sin_a = jnp.concatenate([-sin, zeros_tail], axis=1)
    sin_b = jnp.concatenate([jnp.zeros((n, half), F32), sin, jnp.zeros((n, HEAD_DIM - ROT_DIM), F32)], axis=1)
    return cos_t, sin_a, sin_b


def _lam_init(layer):
    return 0.8 - 0.6 * math.exp(-0.3 * layer)


def kernel(x_prompt, x_sample, state_conv, cache_k, cache_v, page_table, c_prompt, c_sample,
           ada_w, ada_b, norm_mix_g, norm_ff_g, w_ff1, w_ff2,
           conv_w_pw1, conv_b_pw1, conv_w_dw, conv_b_dw, conv_ln_g, conv_ln_b, conv_w_pw2, conv_b_pw2,
           kv_ada_w, kv_ada_b, kv_norm_g, w_k, w_v, k_norm_g,
           w_q, q_norm_g, lambda_q1, lambda_k1, lambda_q2, lambda_k2, subln_g, w_o):
    n_seq, seq, _ = x_prompt.shape
    n_dec = x_sample.shape[0]
    assert x_sample.shape[1] == 1 and n_dec <= SAMPLE_ROWS
    assert seq % TM_PROMPT == 0 and seq % TQ_ATTN == 0
    n_pages = page_table.shape[1]
    assert n_pages % PAGES_PER_STEP == 0
    n_past = n_pages * PAGE_SIZE
    width = N_HEADS * V_DIM

    rows_p = Rows(m=n_seq * seq, tm=TM_PROMPT, tiles_per_seq=seq // TM_PROMPT, per_row=False)
    rows_s = Rows(m=SAMPLE_ROWS, tm=SAMPLE_ROWS, tiles_per_seq=1, per_row=True)
    pad_s = SAMPLE_ROWS - n_dec

    c_all = jnp.concatenate([c_sample, c_prompt,
                             jnp.zeros((SAMPLE_ROWS - n_dec - n_seq, D_MODEL), F32)], axis=0)
    mods_all = _ada(c_all, ada_w, ada_b)
    kv_mods_all = _ada(c_all, kv_ada_w[None], kv_ada_b[None])[0]

    def group_mods(m, rows):
        if rows.per_row:
            return m
        return m[n_dec:n_dec + n_seq].reshape(n_seq, 1, m.shape[-1])

    w_ff1, w_ff2 = w_ff1.astype(BF16), w_ff2.astype(BF16)
    conv_w_pw1, conv_w_pw2 = conv_w_pw1.astype(BF16), conv_w_pw2.astype(BF16)
    w_k, w_v, w_q, w_o = w_k.astype(BF16), w_v.astype(BF16), w_q.astype(BF16), w_o.astype(BF16)

    def trunk(rows, x, rope, conv_fn, attn_fn):
        new_glu = []
        kv_mods = group_mods(kv_mods_all, rows)
        k_f32 = v_f32 = k_att = v_att = None
        for l in range(DEPTH):
            mods = group_mods(mods_all[l], rows)
            if l == N_A_LAYERS:
                k_f32, k_bf = _proj(rows, x, kv_mods, 0, 1, kv_norm_g, w_k[None], 0, (F32, BF16),
                                    rope=rope, head_g=k_norm_g)
                v_f32, v_bf = _proj(rows, x, kv_mods, 0, 1, kv_norm_g, w_v[None], 0, (F32, BF16))
                k_att, v_att = (k_f32, v_f32) if rows.per_row else (k_bf, v_bf)
            if l < N_A_LAYERS:
                glu = _pw1_glu(rows, x, mods, norm_mix_g[l], conv_w_pw1, l, conv_b_pw1[l])
                new_glu.append(glu)
                x = conv_fn(l, glu, x, mods)
            else:
                j = l - N_A_LAYERS
                q_dtype = F32 if rows.per_row else BF16
                (q,) = _proj(rows, x, mods, 0, 1, norm_mix_g[l], w_q, j, (q_dtype,),
                             rope=rope, head_g=q_norm_g[j])
                lam_params = (lambda_q1[j], lambda_k1[j], lambda_q2[j], lambda_k2[j])
                o = attn_fn(q, k_att, v_att, lam_params, subln_g[j], _lam_init(l))
                x = _res_linear(rows, o, w_o, j, x, mods, 2)
            x = _mlp(rows, x, mods, norm_ff_g[l], w_ff1, w_ff2, l)
        return x, new_glu, k_f32, v_f32

    rope_p = _rope_tables(jnp.arange(seq))
    state_p = jnp.zeros((n_seq, CONV_HALO, D_MODEL), F32)

    def conv_prompt(l, glu, x, mods):
        return _conv_pw2_prompt(rows_p, glu, state_p, x, mods, conv_w_dw[l], conv_b_dw[l],
                                conv_ln_g[l], conv_ln_b[l], conv_w_pw2, l, conv_b_pw2[l])

    def attn_prompt(q, k, v, lam_params, sg, lam_init):
        return _attn_prompt(q, k, v, lam_params, sg, lam_init, n_seq, seq)

    y_p, glu_p, k_p, v_p = trunk(rows_p, x_prompt.reshape(n_seq * seq, D_MODEL), rope_p,
                                 conv_prompt, attn_prompt)
    conv_p = jnp.stack([g.reshape(n_seq, seq, D_MODEL)[:, seq - (CONV_WIDTH - 1):] for g in glu_p])

    rope_s = _rope_tables(jnp.full((SAMPLE_ROWS,), n_past, jnp.int32))
    conv_s_rows = []

    def conv_sample(l, glu, x, mods):
        hist = jnp.pad(state_conv[l].transpose(1, 0, 2), ((0, 0), (0, pad_s), (0, 0)))
        full = jnp.concatenate([hist, glu[None]], axis=0)
        conv_s_rows.append(full[1:, :n_dec].transpose(1, 0, 2))
        return _conv_pw2_sample(rows_s, full, x, mods, conv_w_dw[l], conv_b_dw[l],
                                conv_ln_g[l], conv_ln_b[l], conv_w_pw2, l, conv_b_pw2[l])

    def attn_sample(q, k, v, lam_params, sg, lam_init):
        o = _attn_sample(q[:n_dec], k[:n_dec], v[:n_dec], cache_k, cache_v, page_table,
                         lam_params, sg, lam_init)
        return jnp.pad(o, ((0, pad_s), (0, 0)))

    x_s = jnp.pad(x_sample.reshape(n_dec, D_MODEL), ((0, pad_s), (0, 0)))
    y_s, _, k_s, v_s = trunk(rows_s, x_s, rope_s, conv_sample, attn_sample)

    return (y_p.reshape(n_seq, seq, D_MODEL),
            y_s[:n_dec].reshape(n_dec, 1, D_MODEL),
            conv_p,
            jnp.stack(conv_s_rows),
            k_p.reshape(n_seq, seq, N_HEADS, 2, HEAD_DIM),
            v_p.reshape(n_seq, seq, N_HEADS, V_DIM),
            k_s[:n_dec].reshape(n_dec, 1, N_HEADS, 2, HEAD_DIM),
            v_s[:n_dec].reshape(n_dec, 1, N_HEADS, V_DIM))
```

```python
import functools
import math
from typing import NamedTuple

import jax
import jax.numpy as jnp
from jax import lax
from jax.experimental import pallas as pl
from jax.experimental.pallas import tpu as pltpu

F32 = jnp.float32
BF16 = jnp.bfloat16

D_MODEL = 2048
DEPTH = 4
N_A_LAYERS = DEPTH // 2
CONV_WIDTH = 31
HEAD_DIM = 128
V_DIM = 2 * HEAD_DIM
N_HEADS = D_MODEL // V_DIM
ROT_DIM = HEAD_DIM // 4
ROPE_THETA = 500000.0
EPS = 1e-6
SCALE = HEAD_DIM ** -0.5
PAGE_SIZE = 128
NEG_BIG = -1e30

LANE = 128
SUBLANE = 8
VMEM_LIMIT_BYTES = 58 * 1024 * 1024

SAMPLE_ROWS = 16
TM_PROMPT = 512
TN_PROJ = 2048
TN_CONV = 1024
TN_GLU = 512
TF_MLP = 2048
MLP_SPLIT = 4
TF_MLP_EMIT = 512
TM_PW1 = 1024
TN_ADA = 1024
NORM_ROWS = 64
CONV_COLS = 256
CONV_ROWS = 64
CONV_HALO = 32
TQ_ATTN = 2048
TK_ATTN = 1024
ATTN_SUB_ROWS = 512
PAGES_PER_STEP = 8


class Rows(NamedTuple):
    m: int
    tm: int
    tiles_per_seq: int
    per_row: bool


def _cparams(*sem):
    return pltpu.CompilerParams(dimension_semantics=sem, vmem_limit_bytes=VMEM_LIMIT_BYTES)


def _mod_spec(rows, mods, k, width, col_of_j):
    nb = D_MODEL // width
    if rows.per_row:
        return pl.BlockSpec((rows.tm, width), lambda i, j: (0, k * nb + col_of_j(j)))
    return pl.BlockSpec((None, 1, width), lambda i, j: (i // rows.tiles_per_seq, 0, k * nb + col_of_j(j)))


def _rows_of(ref, r0, n):
    if ref.shape[0] == 1:
        return ref[...]
    return ref[pl.ds(r0, n), :]


def _norm_mod_to_bf16(x_ref, g_ref, sc_ref, sh_ref, h_ref):
    tm = x_ref.shape[0]
    rb = min(tm, NORM_ROWS)

    def body(r, carry):
        r0 = pl.multiple_of(r * rb, rb)
        x = x_ref[pl.ds(r0, rb), :]
        y = x * lax.rsqrt(jnp.mean(x * x, axis=-1, keepdims=True) + EPS)
        y = y * g_ref[...]
        h = y * (1.0 + _rows_of(sc_ref, r0, rb)) + _rows_of(sh_ref, r0, rb)
        h_ref[pl.ds(r0, rb), :] = h.astype(BF16)
        return carry

    lax.fori_loop(0, tm // rb, body, 0)


def _dot(a, b):
    return jnp.dot(a, b, preferred_element_type=F32)


def _ada_kernel(c_ref, w_ref, b_ref, o_ref):
    c = c_ref[...]
    s = (c * jax.nn.sigmoid(c)).astype(BF16)
    o_ref[...] = _dot(s, w_ref[...].astype(BF16)) + b_ref[...]


def _ada(c_all, w, b):
    n_layers, _, n = w.shape
    r = c_all.shape[0]
    return pl.pallas_call(
        _ada_kernel,
        out_shape=jax.ShapeDtypeStruct((n_layers, r, n), F32),
        grid=(n_layers, n // TN_ADA),
        in_specs=[
            pl.BlockSpec((r, D_MODEL), lambda l, j: (0, 0)),
            pl.BlockSpec((None, D_MODEL, TN_ADA), lambda l, j: (l, 0, j)),
            pl.BlockSpec((None, 1, TN_ADA), lambda l, j: (l, 0, j)),
        ],
        out_specs=pl.BlockSpec((None, r, TN_ADA), lambda l, j: (l, 0, j)),
        compiler_params=_cparams("parallel", "parallel"),
        name="ada_mod",
    )(c_all, w, b.reshape(n_layers, 1, n))


def _pw1_glu_kernel(x_ref, g_ref, sc_ref, sh_ref, wa_ref, wg_ref, ba_ref, bg_ref, o_ref, h_ref):
    @pl.when(pl.program_id(1) == 0)
    def _():
        _norm_mod_to_bf16(x_ref, g_ref, sc_ref, sh_ref, h_ref)

    h = h_ref[...]
    a = _dot(h, wa_ref[...]) + ba_ref[...]
    gt = _dot(h, wg_ref[...]) + bg_ref[...]
    o_ref[...] = a * jax.nn.sigmoid(gt)


def _pw1_glu(rows, x, mods, norm_g, w, layer, b):
    nj = D_MODEL // TN_GLU
    b2 = b.reshape(1, 2 * D_MODEL)
    full = lambda j: 0
    return pl.pallas_call(
        _pw1_glu_kernel,
        out_shape=jax.ShapeDtypeStruct((rows.m, D_MODEL), F32),
        grid=(rows.m // rows.tm, nj),
        in_specs=[
            pl.BlockSpec((rows.tm, D_MODEL), lambda i, j: (i, 0)),
            pl.BlockSpec((1, D_MODEL), lambda i, j: (0, 0)),
            _mod_spec(rows, mods, 1, D_MODEL, full),
            _mod_spec(rows, mods, 0, D_MODEL, full),
            pl.BlockSpec((None, D_MODEL, TN_GLU), lambda i, j: (layer, 0, j)),
            pl.BlockSpec((None, D_MODEL, TN_GLU), lambda i, j: (layer, 0, j + nj)),
            pl.BlockSpec((1, TN_GLU), lambda i, j: (0, j)),
            pl.BlockSpec((1, TN_GLU), lambda i, j: (0, j + nj)),
        ],
        out_specs=pl.BlockSpec((rows.tm, TN_GLU), lambda i, j: (i, j)),
        scratch_shapes=[pltpu.VMEM((rows.tm, D_MODEL), BF16)],
        compiler_params=_cparams("parallel", "arbitrary"),
        name="pw1_glu",
    )(x, norm_g.reshape(1, D_MODEL), mods, mods, w, w, b2, b2)


def _ln_silu_to_bf16(yc_ref, lng_ref, lnb_ref, y_ref):
    ncb, tm, cw = yc_ref.shape
    rb = min(tm, NORM_ROWS)

    def body(r, carry):
        r0 = pl.multiple_of(r * rb, rb)
        parts = [yc_ref[cb, pl.ds(r0, rb), :] for cb in range(ncb)]
        mu = sum(jnp.sum(p, axis=-1, keepdims=True) for p in parts) * (1.0 / D_MODEL)
        var = sum(jnp.sum(jnp.square(p - mu), axis=-1, keepdims=True) for p in parts) * (1.0 / D_MODEL)
        inv = lax.rsqrt(var + EPS)
        for cb in range(ncb):
            cols = slice(cb * cw, (cb + 1) * cw)
            y = (parts[cb] - mu) * inv * lng_ref[:, cols] + lnb_ref[:, cols]
            y = y * jax.nn.sigmoid(y)
            y_ref[pl.ds(r0, rb), cols] = y.astype(BF16)
        return carry

    lax.fori_loop(0, tm // rb, body, 0)


def _pw2_residual(y_ref, w_ref, b_ref, x_ref, gate_ref, o_ref):
    out = _dot(y_ref[...], w_ref[...]) + b_ref[...]
    o_ref[...] = x_ref[...] + gate_ref[...] * out


def _conv_prompt_kernel(glu_ref, halo_ref, st_ref, wd_ref, bd_ref, lng_ref, lnb_ref,
                        w_ref, b_ref, x_ref, gate_ref, o_ref, win_ref, shf_ref, yc_ref, y_ref, *, tiles_per_seq):
    ncb, win_rows, cw = win_ref.shape
    tm = glu_ref.shape[0]

    @pl.when(pl.program_id(1) == 0)
    def _():
        first = pl.program_id(0) % tiles_per_seq == 0
        for cb in range(ncb):
            cols = slice(cb * cw, (cb + 1) * cw)
            win_ref[cb, 0:CONV_HALO, :] = jnp.where(first, st_ref[:, cols], halo_ref[:, cols])
            win_ref[cb, CONV_HALO:, :] = glu_ref[:, cols]

        base = CONV_HALO - (CONV_WIDTH - 1)

        def conv_cols(cb, carry):
            wd = wd_ref[cb]
            for r in range(1, SUBLANE):
                shf_ref[r - 1, 0:win_rows - SUBLANE, :] = win_ref[cb, r:r + win_rows - SUBLANE, :]
            for rb in range(tm // CONV_ROWS):
                acc = jnp.zeros((CONV_ROWS, cw), F32)
                for w in range(CONV_WIDTH):
                    r = (base + w) % SUBLANE
                    start = rb * CONV_ROWS + (base + w) - r
                    if r == 0:
                        tap = win_ref[cb, pl.ds(start, CONV_ROWS), :]
                    else:
                        tap = shf_ref[r - 1, pl.ds(start, CONV_ROWS), :]
                    acc = acc + tap * wd[w:w + 1, :]
                yc_ref[cb, pl.ds(rb * CONV_ROWS, CONV_ROWS), :] = acc + bd_ref[cb]
            return carry

        lax.fori_loop(0, ncb, conv_cols, 0)
        _ln_silu_to_bf16(yc_ref, lng_ref, lnb_ref, y_ref)

    _pw2_residual(y_ref, w_ref, b_ref, x_ref, gate_ref, o_ref)


def _conv_sample_kernel(full_ref, wd_ref, bd_ref, lng_ref, lnb_ref,
                        w_ref, b_ref, x_ref, gate_ref, o_ref, yc_ref, y_ref):
    ncb, _, cw = yc_ref.shape

    @pl.when(pl.program_id(1) == 0)
    def _():
        for cb in range(ncb):
            cols = slice(cb * cw, (cb + 1) * cw)
            wd = wd_ref[cb]
            acc = jnp.zeros((full_ref.shape[1], cw), F32)
            for w in range(CONV_WIDTH):
                acc = acc + full_ref[w, :, cols] * wd[w:w + 1, :]
            yc_ref[cb] = acc + bd_ref[cb]
        _ln_silu_to_bf16(yc_ref, lng_ref, lnb_ref, y_ref)

    _pw2_residual(y_ref, w_ref, b_ref, x_ref, gate_ref, o_ref)


def _conv_weights(w_dw, b_dw):
    ncb = D_MODEL // CONV_COLS
    wd = jnp.pad(w_dw, ((0, CONV_HALO - CONV_WIDTH), (0, 0)))
    wd = wd.reshape(CONV_HALO, ncb, CONV_COLS).transpose(1, 0, 2)
    return wd, b_dw.reshape(ncb, 1, CONV_COLS)


def _conv_pw2_prompt(rows, glu, state, x, mods, w_dw, b_dw, ln_g, ln_b, w, layer, b):
    ncb = D_MODEL // CONV_COLS
    nj = D_MODEL // TN_CONV
    wd, bd = _conv_weights(w_dw, b_dw)
    halo_per_tile = rows.tm // CONV_HALO
    tps = rows.tiles_per_seq
    const2 = lambda i, j: (0, 0)
    const3 = lambda i, j: (0, 0, 0)
    return pl.pallas_call(
        functools.partial(_conv_prompt_kernel, tiles_per_seq=tps),
        out_shape=jax.ShapeDtypeStruct((rows.m, D_MODEL), F32),
        grid=(rows.m // rows.tm, nj),
        in_specs=[
            pl.BlockSpec((rows.tm, D_MODEL), lambda i, j: (i, 0)),
            pl.BlockSpec((CONV_HALO, D_MODEL), lambda i, j: (jnp.maximum(i * halo_per_tile - 1, 0), 0)),
            pl.BlockSpec((None, CONV_HALO, D_MODEL), lambda i, j: (i // tps, 0, 0)),
            pl.BlockSpec((ncb, CONV_HALO, CONV_COLS), const3),
            pl.BlockSpec((ncb, 1, CONV_COLS), const3),
            pl.BlockSpec((1, D_MODEL), const2),
            pl.BlockSpec((1, D_MODEL), const2),
            pl.BlockSpec((None, D_MODEL, TN_CONV), lambda i, j: (layer, 0, j)),
            pl.BlockSpec((1, TN_CONV), lambda i, j: (0, j)),
            pl.BlockSpec((rows.tm, TN_CONV), lambda i, j: (i, j)),
            _mod_spec(rows, mods, 2, TN_CONV, lambda j: j),
        ],
        out_specs=pl.BlockSpec((rows.tm, TN_CONV), lambda i, j: (i, j)),
        scratch_shapes=[
            pltpu.VMEM((ncb, CONV_HALO + rows.tm, CONV_COLS), F32),
            pltpu.VMEM((SUBLANE - 1, CONV_HALO + rows.tm, CONV_COLS), F32),
            pltpu.VMEM((ncb, rows.tm, CONV_COLS), F32),
            pltpu.VMEM((rows.tm, D_MODEL), BF16),
        ],
        compiler_params=_cparams("parallel", "arbitrary"),
        name="conv_pw2_prompt",
    )(glu, glu, state, wd, bd, ln_g.reshape(1, D_MODEL), ln_b.reshape(1, D_MODEL),
      w, b.reshape(1, D_MODEL), x, mods)


def _conv_pw2_sample(rows, full, x, mods, w_dw, b_dw, ln_g, ln_b, w, layer, b):
    ncb = D_MODEL // CONV_COLS
    nj = D_MODEL // TN_CONV
    wd, bd = _conv_weights(w_dw, b_dw)
    const2 = lambda i, j: (0, 0)
    const3 = lambda i, j: (0, 0, 0)
    return pl.pallas_call(
        _conv_sample_kernel,
        out_shape=jax.ShapeDtypeStruct((rows.m, D_MODEL), F32),
        grid=(rows.m // rows.tm, nj),
        in_specs=[
            pl.BlockSpec((CONV_WIDTH, rows.tm, D_MODEL), lambda i, j: (0, i, 0)),
            pl.BlockSpec((ncb, CONV_HALO, CONV_COLS), const3),
            pl.BlockSpec((ncb, 1, CONV_COLS), const3),
            pl.BlockSpec((1, D_MODEL), const2),
            pl.BlockSpec((1, D_MODEL), const2),
            pl.BlockSpec((None, D_MODEL, TN_CONV), lambda i, j: (layer, 0, j)),
            pl.BlockSpec((1, TN_CONV), lambda i, j: (0, j)),
            pl.BlockSpec((rows.tm, TN_CONV), lambda i, j: (i, j)),
            _mod_spec(rows, mods, 2, TN_CONV, lambda j: j),
        ],
        out_specs=pl.BlockSpec((rows.tm, TN_CONV), lambda i, j: (i, j)),
        scratch_shapes=[
            pltpu.VMEM((ncb, rows.tm, CONV_COLS), F32),
            pltpu.VMEM((rows.tm, D_MODEL), BF16),
        ],
        compiler_params=_cparams("parallel", "arbitrary"),
        name="conv_pw2_sample",
    )(full, wd, bd, ln_g.reshape(1, D_MODEL), ln_b.reshape(1, D_MODEL),
      w, b.reshape(1, D_MODEL), x, mods)


def _mlp_kernel(x_ref, g_ref, sc_ref, sh_ref, gate_ref, w1_ref, w2_ref, o_ref, *rest, split, emit):
    if emit:
        w1b_ref, w2b_ref, h_ref = rest
    else:
        (h_ref,) = rest
    f = pl.program_id(1)

    @pl.when(f == 0)
    def _():
        _norm_mod_to_bf16(x_ref, g_ref, sc_ref, sh_ref, h_ref)

    h = h_ref[...]
    sub = w1_ref.shape[1] // split
    part = None
    for s in range(split):
        cs = slice(s * sub, (s + 1) * sub)
        w1, w2 = w1_ref[:, cs], w2_ref[cs, :]
        if emit:
            w1, w2 = w1.astype(BF16), w2.astype(BF16)
            w1b_ref[:, cs] = w1
            w2b_ref[cs, :] = w2
        hid = _dot(h, w1)
        hid = jnp.square(jnp.maximum(hid, 0.0)).astype(BF16)
        contrib = _dot(hid, w2)
        part = contrib if part is None else part + contrib

    @pl.when(f == 0)
    def _():
        o_ref[...] = part

    @pl.when(f > 0)
    def _():
        o_ref[...] += part

    @pl.when(f == pl.num_programs(1) - 1)
    def _():
        o_ref[...] = x_ref[...] + gate_ref[...] * o_ref[...]


def _mlp(rows, x, mods, norm_g, w1, w2, layer, emit=False):
    d_ff = w1.shape[2]
    full = lambda j: 0
    if emit:
        assert rows.m == rows.tm and w1.dtype == F32 and w2.dtype == F32
        tf, split = TF_MLP_EMIT, 1
    else:
        assert w1.dtype == BF16 and w2.dtype == BF16
        tf, split = TF_MLP, MLP_SPLIT
    out_shape = [jax.ShapeDtypeStruct((rows.m, D_MODEL), F32)]
    out_specs = [pl.BlockSpec((rows.tm, D_MODEL), lambda i, f: (i, 0))]
    if emit:
        out_shape += [jax.ShapeDtypeStruct((D_MODEL, d_ff), BF16), jax.ShapeDtypeStruct((d_ff, D_MODEL), BF16)]
        out_specs += [pl.BlockSpec((D_MODEL, tf), lambda i, f: (0, f)), pl.BlockSpec((tf, D_MODEL), lambda i, f: (f, 0))]
    outs = pl.pallas_call(
        functools.partial(_mlp_kernel, split=split, emit=emit),
        out_shape=out_shape,
        grid=(rows.m // rows.tm, d_ff // tf),
        in_specs=[
            pl.BlockSpec((rows.tm, D_MODEL), lambda i, f: (i, 0)),
            pl.BlockSpec((1, D_MODEL), lambda i, f: (0, 0)),
            _mod_spec(rows, mods, 4, D_MODEL, full),
            _mod_spec(rows, mods, 3, D_MODEL, full),
            _mod_spec(rows, mods, 5, D_MODEL, full),
            pl.BlockSpec((None, D_MODEL, tf), lambda i, f: (layer, 0, f)),
            pl.BlockSpec((None, tf, D_MODEL), lambda i, f: (layer, f, 0)),
        ],
        out_specs=out_specs,
        scratch_shapes=[pltpu.VMEM((rows.tm, D_MODEL), BF16)],
        compiler_params=_cparams("parallel", "arbitrary"),
        name="sqrelu_mlp_emit" if emit else "sqrelu_mlp",
    )(x, norm_g.reshape(1, D_MODEL), mods, mods, mods, w1, w2)
    return outs if emit else outs[0]


def _proj_kernel(*refs, qk_norm, n_out):
    if qk_norm:
        x_ref, g_ref, sc_ref, sh_ref, w_ref, ng_ref, cos_ref, sa_ref, sb_ref = refs[:9]
        rest = refs[9:]
    else:
        x_ref, g_ref, sc_ref, sh_ref, w_ref = refs[:5]
        rest = refs[5:]
    out_refs, h_ref = rest[:n_out], rest[n_out]

    @pl.when(pl.program_id(1) == 0)
    def _():
        _norm_mod_to_bf16(x_ref, g_ref, sc_ref, sh_ref, h_ref)

    acc = _dot(h_ref[...], w_ref[...])
    tn = acc.shape[1]
    if not qk_norm:
        for o_ref in out_refs:
            o_ref[...] = acc.astype(o_ref.dtype)
        return

    cos, sa, sb, ng = cos_ref[...], sa_ref[...], sb_ref[...], ng_ref[...]
    for grp in range(tn // HEAD_DIM):
        cols = slice(grp * HEAD_DIM, (grp + 1) * HEAD_DIM)
        a = acc[:, cols]
        y = a * lax.rsqrt(jnp.mean(a * a, axis=-1, keepdims=True) + EPS) * ng
        y = (y * cos + pltpu.roll(y, HEAD_DIM - ROT_DIM // 2, 1) * sa
             + pltpu.roll(y, ROT_DIM // 2, 1) * sb)
        for o_ref in out_refs:
            if o_ref.shape[1] == tn:
                o_ref[:, cols] = y.astype(o_ref.dtype)
            else:
                o_ref[pl.ds(grp, acc.shape[0], stride=tn // HEAD_DIM), :] = y.astype(o_ref.dtype)


def _proj(rows, x, mods, k_shift, k_scale, norm_g, w, layer, out_dtypes, rope=None, head_g=None,
          head_major_f32=False):
    n = w.shape[2]
    full = lambda j: 0
    qk_norm = rope is not None
    in_specs = [
        pl.BlockSpec((rows.tm, D_MODEL), lambda i, j: (i, 0)),
        pl.BlockSpec((1, D_MODEL), lambda i, j: (0, 0)),
        _mod_spec(rows, mods, k_scale, D_MODEL, full),
        _mod_spec(rows, mods, k_shift, D_MODEL, full),
        pl.BlockSpec((None, D_MODEL, TN_PROJ), lambda i, j: (layer, 0, j)),
    ]
    args = [x, norm_g.reshape(1, D_MODEL), mods, mods, w]
    if qk_norm:
        tps = rows.tiles_per_seq
        in_specs.append(pl.BlockSpec((1, HEAD_DIM), lambda i, j: (0, 0)))
        args.append(head_g.reshape(1, HEAD_DIM))
        for t in rope:
            in_specs.append(pl.BlockSpec((rows.tm, HEAD_DIM), lambda i, j: (i % tps, 0)))
            args.append(t)
    out_shape, out_specs = [], []
    for dt in out_dtypes:
        if head_major_f32 and dt == F32:
            assert qk_norm and n == TN_PROJ
            groups = n // HEAD_DIM
            out_shape.append(jax.ShapeDtypeStruct((rows.m * groups, HEAD_DIM), dt))
            out_specs.append(pl.BlockSpec((rows.tm * groups, HEAD_DIM), lambda i, j: (i, 0)))
        else:
            out_shape.append(jax.ShapeDtypeStruct((rows.m, n), dt))
            out_specs.append(pl.BlockSpec((rows.tm, TN_PROJ), lambda i, j: (i, j)))
    outs = pl.pallas_call(
        functools.partial(_proj_kernel, qk_norm=qk_norm, n_out=len(out_dtypes)),
        out_shape=out_shape,
        grid=(rows.m // rows.tm, n // TN_PROJ),
        in_specs=in_specs,
        out_specs=out_specs,
        scratch_shapes=[pltpu.VMEM((rows.tm, D_MODEL), BF16)],
        compiler_params=_cparams("parallel", "arbitrary"),
        name="norm_proj_rope" if qk_norm else "norm_proj",
    )(*args)
    return outs


def _res_linear_kernel(a_ref, w_ref, x_ref, gate_ref, o_ref):
    out = _dot(a_ref[...].astype(BF16), w_ref[...])
    o_ref[...] = x_ref[...] + gate_ref[...] * out


def _res_linear(rows, a, w, layer, x, mods, k_gate):
    _, kdim, n = w.shape
    return pl.pallas_call(
        _res_linear_kernel,
        out_shape=jax.ShapeDtypeStruct((rows.m, n), F32),
        grid=(rows.m // rows.tm, n // TN_PROJ),
        in_specs=[
            pl.BlockSpec((rows.tm, kdim), lambda i, j: (i, 0)),
            pl.BlockSpec((None, kdim, TN_PROJ), lambda i, j: (layer, 0, j)),
            pl.BlockSpec((rows.tm, TN_PROJ), lambda i, j: (i, j)),
            _mod_spec(rows, mods, k_gate, TN_PROJ, lambda j: j),
        ],
        out_specs=pl.BlockSpec((rows.tm, TN_PROJ), lambda i, j: (i, j)),
        compiler_params=_cparams("parallel", "arbitrary"),
        name="out_proj_residual",
    )(a, w, x, mods)


def _lambda_full(lq1_ref, lk1_ref, lq2_ref, lk2_ref, lam_init):
    d1 = jnp.sum(lq1_ref[...] * lk1_ref[...], axis=-1, keepdims=True)
    d2 = jnp.sum(lq2_ref[...] * lk2_ref[...], axis=-1, keepdims=True)
    return jnp.exp(d1) - jnp.exp(d2) + lam_init


def _subln(o, g, lam_init):
    y = o * lax.rsqrt(jnp.mean(o * o, axis=-1, keepdims=True) + EPS)
    return y * g * (1.0 - lam_init)


def _attn_prompt_kernel(q_ref, k_ref, v_ref, lq1_ref, lk1_ref, lq2_ref, lk2_ref, sg_ref,
                        o_ref, m_ref, l_ref, acc_ref, *, lam_init):
    tq = q_ref.shape[0]
    tk, rs = TK_ATTN, ATTN_SUB_ROWS
    i = pl.program_id(2)
    m_ref[...] = jnp.full(m_ref.shape, NEG_BIG, F32)
    l_ref[...] = jnp.zeros(l_ref.shape, F32)
    acc_ref[...] = jnp.zeros(acc_ref.shape, F32)
    c2 = SCALE * math.log2(math.e)

    def chunk(c, diag):
        k0 = pl.multiple_of(c * tk, tk)
        vc = v_ref[pl.ds(k0, tk), :]
        for comp in range(2):
            cols = slice(comp * HEAD_DIM, (comp + 1) * HEAD_DIM)
            kc = k_ref[pl.ds(k0, tk), cols]
            for r in range(tq // rs):
                if diag is not None and (r + 1) * rs <= diag * tk:
                    continue
                rows = slice(r * rs, (r + 1) * rs)
                s = lax.dot_general(q_ref[rows, cols], kc, (((1,), (1,)), ((), ())),
                                    preferred_element_type=F32)
                if diag is not None:
                    row = lax.broadcasted_iota(jnp.int32, s.shape, 0) + r * rs
                    col = lax.broadcasted_iota(jnp.int32, s.shape, 1) + diag * tk
                    s = jnp.where(col <= row, s, NEG_BIG)
                tiles = [s[:, t * LANE:(t + 1) * LANE] for t in range(tk // LANE)]
                mx = functools.reduce(jnp.maximum, tiles)
                m_old = m_ref[comp, rows, :]
                m_new = jnp.maximum(m_old, jnp.max(mx, axis=-1, keepdims=True))
                alpha = jnp.exp2((m_old - m_new) * c2)
                ps = [jnp.exp2((t - m_new) * c2) for t in tiles]
                psum = functools.reduce(lambda a, b: a + b, ps)
                l_ref[comp, rows, :] = alpha * l_ref[comp, rows, :] + jnp.sum(psum, axis=-1, keepdims=True)
                m_ref[comp, rows, :] = m_new
                pv = _dot(jnp.concatenate([t.astype(BF16) for t in ps], axis=1), vc)
                for t in range(V_DIM // LANE):
                    lc = slice(t * LANE, (t + 1) * LANE)
                    acc_ref[comp, rows, lc] = alpha * acc_ref[comp, rows, lc] + pv[:, lc]

    def full_chunk(c, carry):
        chunk(c, None)
        return carry

    nd = tq // tk
    lax.fori_loop(0, i * nd, full_chunk, 0)
    for d in range(nd):
        chunk(i * nd + d, d)

    lam = _lambda_full(lq1_ref, lk1_ref, lq2_ref, lk2_ref, lam_init)
    outs = []
    for t in range(V_DIM // LANE):
        lc = slice(t * LANE, (t + 1) * LANE)
        outs.append(acc_ref[0, :, lc] / l_ref[0] - lam * (acc_ref[1, :, lc] / l_ref[1]))
    ms = sum(jnp.sum(o * o, axis=-1, keepdims=True) for o in outs) * (1.0 / V_DIM)
    inv = lax.rsqrt(ms + EPS) * (1.0 - lam_init)
    for t in range(V_DIM // LANE):
        lc = slice(t * LANE, (t + 1) * LANE)
        o_ref[:, lc] = (outs[t] * inv * sg_ref[:, lc]).astype(o_ref.dtype)


def _attn_prompt(q, k, v, lam_params, subln_g, lam_init, n_seq, seq):
    nq = seq // TQ_ATTN
    lam_spec = pl.BlockSpec((1, HEAD_DIM), lambda b, h, i: (0, 0))
    return pl.pallas_call(
        functools.partial(_attn_prompt_kernel, lam_init=lam_init),
        out_shape=jax.ShapeDtypeStruct(q.shape, BF16),
        grid=(n_seq, N_HEADS, nq),
        in_specs=[
            pl.BlockSpec((TQ_ATTN, V_DIM), lambda b, h, i: (b * nq + i, h)),
            pl.BlockSpec((seq, V_DIM), lambda b, h, i: (b, h)),
            pl.BlockSpec((seq, V_DIM), lambda b, h, i: (b, h)),
            lam_spec, lam_spec, lam_spec, lam_spec,
            pl.BlockSpec((1, V_DIM), lambda b, h, i: (0, 0)),
        ],
        out_specs=pl.BlockSpec((TQ_ATTN, V_DIM), lambda b, h, i: (b * nq + i, h)),
        scratch_shapes=[
            pltpu.VMEM((2, TQ_ATTN, LANE), F32),
            pltpu.VMEM((2, TQ_ATTN, LANE), F32),
            pltpu.VMEM((2, TQ_ATTN, V_DIM), F32),
        ],
        compiler_params=_cparams("parallel", "parallel", "arbitrary"),
        name="diff_attn_prompt",
    )(q, k, v, *[p.reshape(1, HEAD_DIM) for p in lam_params], subln_g.reshape(1, V_DIM))


def _attn_sample_kernel(pt_ref, q_ref, kn_ref, vn_ref, lq1_ref, lk1_ref, lq2_ref, lk2_ref, sg_ref,
                        *refs, lam_init):
    npg = PAGES_PER_STEP
    k_refs, v_refs = refs[:npg], refs[npg:2 * npg]
    o_ref, m_ref, l_ref, acc_ref = refs[2 * npg:]
    c = pl.program_id(1)
    nr = 2 * N_HEADS
    flat = k_refs[0].shape[0]

    @pl.when(c == 0)
    def _():
        m_ref[...] = jnp.full(m_ref.shape, NEG_BIG, F32)
        l_ref[...] = jnp.zeros(l_ref.shape, F32)
        acc_ref[...] = jnp.zeros(acc_ref.shape, F32)

    q = q_ref[...]
    q_bf = q.astype(BF16)
    row = lax.broadcasted_iota(jnp.int32, (nr, flat), 0)
    lane = lax.broadcasted_iota(jnp.int32, (nr, flat), 1)
    own = (N_HEADS - 1 - jnp.bitwise_and(row, N_HEADS - 1)) * 2 + jnp.right_shift(row, 3)
    valid = jnp.bitwise_and(lane, nr - 1) == own

    def align_to_values(pr):
        blocks = []
        for half in range(2):
            for comp in range(2):
                base = (half * N_HEADS - (N_HEADS - 1) - comp) % LANE
                tiles = [pltpu.roll(pr[comp * N_HEADS:(comp + 1) * N_HEADS, t * LANE:(t + 1) * LANE],
                                    base, 1, stride=1, stride_axis=0) for t in range(flat // LANE)]
                blocks.append(jnp.concatenate(tiles, axis=1))
        return jnp.concatenate(blocks, axis=0)

    m_old = m_ref[...]
    m_new = m_old
    s_pages = []
    for p in range(npg):
        s = lax.dot_general(q_bf, k_refs[p][...].astype(BF16), (((1,), (1,)), ((), ())),
                            preferred_element_type=F32) * SCALE
        s = jnp.where(valid, s, NEG_BIG)
        s_pages.append(s)
        m_new = jnp.maximum(m_new, jnp.max(s, axis=-1, keepdims=True))
    alpha = jnp.exp(m_old - m_new)
    l_new = alpha * l_ref[...]
    pv = jnp.zeros(acc_ref.shape, F32)
    for p in range(npg):
        pr = jnp.exp(s_pages[p] - m_new)
        l_new = l_new + jnp.sum(pr, axis=-1, keepdims=True)
        pv = pv + _dot(align_to_values(pr).astype(BF16), v_refs[p][...].astype(BF16))
    acc_ref[...] = jnp.concatenate([alpha, alpha], axis=0) * acc_ref[...] + pv
    l_ref[...] = l_new
    m_ref[...] = m_new

    @pl.when(c == pl.num_programs(1) - 1)
    def _():
        s_new = jnp.sum(q * kn_ref[...], axis=-1, keepdims=True) * SCALE
        m_fin = jnp.maximum(m_ref[...], s_new)
        a = jnp.exp(m_ref[...] - m_fin)
        p_new = jnp.exp(s_new - m_fin)
        l_fin = a * l_ref[...] + p_new
        two = lambda x: jnp.concatenate([x, x], axis=0)
        w_all = (two(a) * acc_ref[...] + two(p_new) * vn_ref[...]) / two(l_fin)
        lam = _lambda_full(lq1_ref, lk1_ref, lq2_ref, lk2_ref, lam_init)
        halves = [w_all[half * nr:half * nr + N_HEADS] - lam * w_all[half * nr + N_HEADS:(half + 1) * nr]
                  for half in range(2)]
        ms = sum(jnp.sum(d * d, axis=-1, keepdims=True) for d in halves) * (1.0 / V_DIM)
        inv = lax.rsqrt(ms + EPS) * (1.0 - lam_init)
        for half in range(2):
            o_ref[half * N_HEADS:(half + 1) * N_HEADS, :] = (
                halves[half] * inv * sg_ref[:, half * LANE:(half + 1) * LANE])


def _attn_sample(q, k_new, v_new, cache_k, cache_v, page_table, lam_params, subln_g, lam_init):
    n_seq, n_pages = page_table.shape
    n_pool = cache_k.shape[0]
    nr = 2 * N_HEADS
    flat = PAGE_SIZE * nr
    npg = PAGES_PER_STEP
    n_chunks = n_pages // npg

    k_flat = cache_k.reshape(n_pool, flat, HEAD_DIM)
    v_flat = cache_v.reshape(n_pool, PAGE_SIZE, N_HEADS, 2, LANE).transpose(0, 1, 3, 2, 4).reshape(n_pool, flat, LANE)

    def qk_rows(x):
        return x.reshape(n_seq, N_HEADS, 2, HEAD_DIM).transpose(0, 2, 1, 3)[:, :, ::-1].reshape(n_seq, nr, HEAD_DIM)

    v_rows = v_new.reshape(n_seq, N_HEADS, 2, LANE).transpose(0, 2, 1, 3)[:, :, ::-1]
    v_rows = jnp.broadcast_to(v_rows[:, :, None], (n_seq, 2, 2, N_HEADS, LANE)).reshape(n_seq, 2 * nr, LANE)

    row_spec = pl.BlockSpec((None, nr, HEAD_DIM), lambda b, c, pt: (b, 0, 0))
    lam_spec = pl.BlockSpec((1, HEAD_DIM), lambda b, c, pt: (0, 0))

    def page_spec(p):
        return pl.BlockSpec((None, flat, LANE), lambda b, c, pt: (pt[b * n_pages + c * npg + p], 0, 0))

    grid_spec = pltpu.PrefetchScalarGridSpec(
        num_scalar_prefetch=1,
        grid=(n_seq, n_chunks),
        in_specs=[row_spec, row_spec, pl.BlockSpec((None, 2 * nr, LANE), lambda b, c, pt: (b, 0, 0)),
                  lam_spec, lam_spec, lam_spec, lam_spec,
                  pl.BlockSpec((1, V_DIM), lambda b, c, pt: (0, 0))]
                 + [page_spec(p) for p in range(npg)] + [page_spec(p) for p in range(npg)],
        out_specs=row_spec,
        scratch_shapes=[
            pltpu.VMEM((nr, 1), F32),
            pltpu.VMEM((nr, 1), F32),
            pltpu.VMEM((2 * nr, LANE), F32),
        ],
    )
    out = pl.pallas_call(
        functools.partial(_attn_sample_kernel, lam_init=lam_init),
        out_shape=jax.ShapeDtypeStruct((n_seq, nr, LANE), F32),
        grid_spec=grid_spec,
        compiler_params=_cparams("parallel", "arbitrary"),
        name="diff_attn_sample",
    )(page_table.reshape(-1), qk_rows(q), qk_rows(k_new), v_rows,
      *[p.reshape(1, HEAD_DIM) for p in lam_params], subln_g.reshape(1, V_DIM),
      *([k_flat] * npg), *([v_flat] * npg))
    return out.reshape(n_seq, 2, N_HEADS, LANE)[:, :, ::-1].transpose(0, 2, 1, 3).reshape(n_seq, N_HEADS * V_DIM)


def _rope_tables(pos):
    half = ROT_DIM // 2
    inv_freq = jnp.power(ROPE_THETA, -jnp.arange(half, dtype=F32) * 2.0 / ROT_DIM)
    ang = pos.astype(F32)[:, None] * inv_freq[None, :]
    cos, sin = jnp.cos(ang), jnp.sin(ang)
    n = pos.shape[0]
    ones = jnp.ones((n, HEAD_DIM - ROT_DIM), F32)
    zeros_tail = jnp.zeros((n, HEAD_DIM - half), F32)
    cos_t = jnp.concatenate([cos, cos, ones], axis=1)
    sin_a = jnp.concatenate([-sin, zeros_tail], axis=1)
    sin_b = jnp.concatenate([jnp.zeros((n, half), F32), sin, jnp.zeros((n, HEAD_DIM - ROT_DIM), F32)], axis=1)
    return cos_t, sin_a, sin_b


def _lam_init(layer):
    return 0.8 - 0.6 * math.exp(-0.3 * layer)


def kernel(x_prompt, x_sample, state_conv, cache_k, cache_v, page_table, c_prompt, c_sample,
           ada_w, ada_b, norm_mix_g, norm_ff_g, w_ff1, w_ff2,
           conv_w_pw1, conv_b_pw1, conv_w_dw, conv_b_dw, conv_ln_g, conv_ln_b, conv_w_pw2, conv_b_pw2,
           kv_ada_w, kv_ada_b, kv_norm_g, w_k, w_v, k_norm_g,
           w_q, q_norm_g, lambda_q1, lambda_k1, lambda_q2, lambda_k2, subln_g, w_o):
    n_seq, seq, _ = x_prompt.shape
    n_dec = x_sample.shape[0]
    assert x_sample.shape[1] == 1 and n_dec <= SAMPLE_ROWS
    assert seq % TM_PROMPT == 0 and seq % TM_PW1 == 0 and seq % TQ_ATTN == 0
    n_pages = page_table.shape[1]
    assert n_pages % PAGES_PER_STEP == 0
    n_past = n_pages * PAGE_SIZE
    width = N_HEADS * V_DIM

    rows_p = Rows(m=n_seq * seq, tm=TM_PROMPT, tiles_per_seq=seq // TM_PROMPT, per_row=False)
    rows_s = Rows(m=SAMPLE_ROWS, tm=SAMPLE_ROWS, tiles_per_seq=1, per_row=True)
    pad_s = SAMPLE_ROWS - n_dec

    c_all = jnp.concatenate([c_sample, c_prompt,
                             jnp.zeros((SAMPLE_ROWS - n_dec - n_seq, D_MODEL), F32)], axis=0)
    mods_all = _ada(c_all, ada_w, ada_b)
    kv_mods_all = _ada(c_all, kv_ada_w[None], kv_ada_b[None])[0]

    def group_mods(m, rows):
        if rows.per_row:
            return m
        return m[n_dec:n_dec + n_seq].reshape(n_seq, 1, m.shape[-1])

    conv_w_pw1, conv_w_pw2 = conv_w_pw1.astype(BF16), conv_w_pw2.astype(BF16)
    w_k, w_v, w_q, w_o = w_k.astype(BF16), w_v.astype(BF16), w_q.astype(BF16), w_o.astype(BF16)

    def trunk(rows, x, rope, conv_fn, attn_fn, ff_bf16):
        new_glu, ff_out = [], []
        kv_mods = group_mods(kv_mods_all, rows)
        rows_pw1 = rows if rows.per_row else rows._replace(
            tm=TM_PW1, tiles_per_seq=rows.tiles_per_seq * rows.tm // TM_PW1)
        groups = N_HEADS * 2
        k_f32 = v_f32 = k_att = v_att = None
        for l in range(DEPTH):
            mods = group_mods(mods_all[l], rows)
            if l == N_A_LAYERS:
                k_f32, k_bf = _proj(rows, x, kv_mods, 0, 1, kv_norm_g, w_k[None], 0, (F32, BF16),
                                    rope=rope, head_g=k_norm_g, head_major_f32=True)
                v_f32, v_bf = _proj(rows, x, kv_mods, 0, 1, kv_norm_g, w_v[None], 0, (F32, BF16))
                k_att, v_att = (k_f32.reshape(rows.m, groups * HEAD_DIM), v_f32) if rows.per_row else (k_bf, v_bf)
            if l < N_A_LAYERS:
                glu = _pw1_glu(rows_pw1, x, mods, norm_mix_g[l], conv_w_pw1, l, conv_b_pw1[l])
                new_glu.append(glu)
                x = conv_fn(l, glu, x, mods)
            else:
                j = l - N_A_LAYERS
                q_dtype = F32 if rows.per_row else BF16
                (q,) = _proj(rows, x, mods, 0, 1, norm_mix_g[l], w_q, j, (q_dtype,),
                             rope=rope, head_g=q_norm_g[j])
                lam_params = (lambda_q1[j], lambda_k1[j], lambda_q2[j], lambda_k2[j])
                o = attn_fn(q, k_att, v_att, lam_params, subln_g[j], _lam_init(l))
                x = _res_linear(rows, o, w_o, j, x, mods, 2)
            if ff_bf16 is None:
                x, w1b, w2b = _mlp(rows, x, mods, norm_ff_g[l], w_ff1, w_ff2, l, emit=True)
                ff_out.append((w1b, w2b))
            else:
                w1b, w2b = ff_bf16[l]
                x = _mlp(rows, x, mods, norm_ff_g[l], w1b[None], w2b[None], 0)
        return x, new_glu, k_f32, v_f32, ff_out

    rope_s = _rope_tables(jnp.full((SAMPLE_ROWS,), n_past, jnp.int32))
    conv_s_rows = []

    def conv_sample(l, glu, x, mods):
        hist = jnp.pad(state_conv[l].transpose(1, 0, 2), ((0, 0), (0, pad_s), (0, 0)))
        full = jnp.concatenate([hist, glu[None]], axis=0)
        conv_s_rows.append(full[1:, :n_dec].transpose(1, 0, 2))
        return _conv_pw2_sample(rows_s, full, x, mods, conv_w_dw[l], conv_b_dw[l],
                                conv_ln_g[l], conv_ln_b[l], conv_w_pw2, l, conv_b_pw2[l])

    def attn_sample(q, k, v, lam_params, sg, lam_init):
        o = _attn_sample(q[:n_dec], k[:n_dec], v[:n_dec], cache_k, cache_v, page_table,
                         lam_params, sg, lam_init)
        return jnp.pad(o, ((0, pad_s), (0, 0)))

    x_s = jnp.pad(x_sample.reshape(n_dec, D_MODEL), ((0, pad_s), (0, 0)))
    y_s, _, k_s, v_s, ff_bf16 = trunk(rows_s, x_s, rope_s, conv_sample, attn_sample, None)

    rope_p = _rope_tables(jnp.arange(seq))
    state_p = jnp.zeros((n_seq, CONV_HALO, D_MODEL), F32)

    def conv_prompt(l, glu, x, mods):
        return _conv_pw2_prompt(rows_p, glu, state_p, x, mods, conv_w_dw[l], conv_b_dw[l],
                                conv_ln_g[l], conv_ln_b[l], conv_w_pw2, l, conv_b_pw2[l])

    def attn_prompt(q, k, v, lam_params, sg, lam_init):
        return _attn_prompt(q, k, v, lam_params, sg, lam_init, n_seq, seq)

    y_p, glu_p, k_p, v_p, _ = trunk(rows_p, x_prompt.reshape(n_seq * seq, D_MODEL), rope_p,
                                    conv_prompt, attn_prompt, ff_bf16)
    conv_p = jnp.stack([g.reshape(n_seq, seq, D_MODEL)[:, seq - (CONV_WIDTH - 1):] for g in glu_p])

    return (y_p.reshape(n_seq, seq, D_MODEL),
            y_s[:n_dec].reshape(n_dec, 1, D_MODEL),
            conv_p,
            jnp.stack(conv_s_rows),
            k_p.reshape(n_seq, seq, N_HEADS, 2, HEAD_DIM),
            v_p.reshape(n_seq, seq, N_HEADS, V_DIM),
            k_s.reshape(SAMPLE_ROWS, N_HEADS, 2, HEAD_DIM)[:n_dec].reshape(n_dec, 1, N_HEADS, 2, HEAD_DIM),
            v_s[:n_dec].reshape(n_dec, 1, N_HEADS, V_DIM))
```

```python
import functools
import math
from typing import NamedTuple

import jax
import jax.numpy as jnp
from jax import lax
from jax.experimental import pallas as pl
from jax.experimental.pallas import tpu as pltpu

F32 = jnp.float32
BF16 = jnp.bfloat16

D_MODEL = 2048
DEPTH = 4
N_A_LAYERS = DEPTH // 2
CONV_WIDTH = 31
HEAD_DIM = 128
V_DIM = 2 * HEAD_DIM
N_HEADS = D_MODEL // V_DIM
ROT_DIM = HEAD_DIM // 4
ROPE_THETA = 500000.0
EPS = 1e-6
SCALE = HEAD_DIM ** -0.5
PAGE_SIZE = 128
NEG_BIG = -1e30

LANE = 128
SUBLANE = 8
VMEM_LIMIT_BYTES = 58 * 1024 * 1024

SAMPLE_ROWS = 16
TM_PROMPT = 512
TN_PROJ = 2048
TN_CONV = 1024
TN_GLU = 512
TF_MLP = 2048
MLP_SPLIT = 4
TF_MLP_EMIT = 512
TM_PW1 = 1024
TN_ADA = 1024
NORM_ROWS = 64
NORM_GROUP_ROWS = 16
NORM_COLS = 512
NORM_UNROLL = 4
CONV_COLS = 256
CONV_ROWS = 64
CONV_HALO = 32
TQ_ATTN = 2048
TK_ATTN = 1024
ATTN_SUB_ROWS = 512
PAGES_PER_STEP = 8


class Rows(NamedTuple):
    m: int
    tm: int
    tiles_per_seq: int
    per_row: bool


def _cparams(*sem):
    return pltpu.CompilerParams(dimension_semantics=sem, vmem_limit_bytes=VMEM_LIMIT_BYTES)


def _mod_spec(rows, mods, k, width, col_of_j):
    nb = D_MODEL // width
    if rows.per_row:
        return pl.BlockSpec((rows.tm, width), lambda i, j: (0, k * nb + col_of_j(j)))
    return pl.BlockSpec((None, 1, width), lambda i, j: (i // rows.tiles_per_seq, 0, k * nb + col_of_j(j)))


def _rows_cols_of(ref, r0, n, cols):
    if ref.shape[0] == 1:
        return ref[:, cols]
    return ref[pl.ds(r0, n), cols]


def _norm_mod_to_bf16(x_ref, g_ref, sc_ref, sh_ref, h_ref):
    tm, d = x_ref.shape
    rb = NORM_GROUP_ROWS
    blocks = [slice(c, c + NORM_COLS) for c in range(0, d, NORM_COLS)]

    def body(r, carry):
        r0 = pl.multiple_of(r * rb, rb)
        ss = jnp.zeros((rb, 1), F32)
        for cols in blocks:
            xc = x_ref[pl.ds(r0, rb), cols]
            ss = ss + jnp.sum(xc * xc, axis=-1, keepdims=True)
        inv = lax.rsqrt(ss * (1.0 / d) + EPS)
        for cols in blocks:
            gm = g_ref[:, cols] * (1.0 + _rows_cols_of(sc_ref, r0, rb, cols))
            h = x_ref[pl.ds(r0, rb), cols] * inv * gm + _rows_cols_of(sh_ref, r0, rb, cols)
            h_ref[pl.ds(r0, rb), cols] = h.astype(BF16)
        return carry

    n = tm // rb
    lax.fori_loop(0, n, body, 0, unroll=min(NORM_UNROLL, n))


def _dot(a, b):
    return jnp.dot(a, b, preferred_element_type=F32)


def _ada_kernel(c_ref, w_ref, b_ref, o_ref):
    c = c_ref[...]
    s = (c * jax.nn.sigmoid(c)).astype(BF16)
    o_ref[...] = _dot(s, w_ref[...].astype(BF16)) + b_ref[...]


def _ada(c_all, w, b):
    n_layers, _, n = w.shape
    r = c_all.shape[0]
    return pl.pallas_call(
        _ada_kernel,
        out_shape=jax.ShapeDtypeStruct((n_layers, r, n), F32),
        grid=(n_layers, n // TN_ADA),
        in_specs=[
            pl.BlockSpec((r, D_MODEL), lambda l, j: (0, 0)),
            pl.BlockSpec((None, D_MODEL, TN_ADA), lambda l, j: (l, 0, j)),
            pl.BlockSpec((None, 1, TN_ADA), lambda l, j: (l, 0, j)),
        ],
        out_specs=pl.BlockSpec((None, r, TN_ADA), lambda l, j: (l, 0, j)),
        compiler_params=_cparams("parallel", "parallel"),
        name="ada_mod",
    )(c_all, w, b.reshape(n_layers, 1, n))


def _pw1_glu_kernel(x_ref, g_ref, sc_ref, sh_ref, wa_ref, wg_ref, ba_ref, bg_ref, o_ref, h_ref):
    @pl.when(pl.program_id(1) == 0)
    def _():
        _norm_mod_to_bf16(x_ref, g_ref, sc_ref, sh_ref, h_ref)

    h = h_ref[...]
    a = _dot(h, wa_ref[...]) + ba_ref[...]
    gt = _dot(h, wg_ref[...]) + bg_ref[...]
    o_ref[...] = a * jax.nn.sigmoid(gt)


def _pw1_glu(rows, x, mods, norm_g, w, layer, b):
    nj = D_MODEL // TN_GLU
    b2 = b.reshape(1, 2 * D_MODEL)
    full = lambda j: 0
    return pl.pallas_call(
        _pw1_glu_kernel,
        out_shape=jax.ShapeDtypeStruct((rows.m, D_MODEL), F32),
        grid=(rows.m // rows.tm, nj),
        in_specs=[
            pl.BlockSpec((rows.tm, D_MODEL), lambda i, j: (i, 0)),
            pl.BlockSpec((1, D_MODEL), lambda i, j: (0, 0)),
            _mod_spec(rows, mods, 1, D_MODEL, full),
            _mod_spec(rows, mods, 0, D_MODEL, full),
            pl.BlockSpec((None, D_MODEL, TN_GLU), lambda i, j: (layer, 0, j)),
            pl.BlockSpec((None, D_MODEL, TN_GLU), lambda i, j: (layer, 0, j + nj)),
            pl.BlockSpec((1, TN_GLU), lambda i, j: (0, j)),
            pl.BlockSpec((1, TN_GLU), lambda i, j: (0, j + nj)),
        ],
        out_specs=pl.BlockSpec((rows.tm, TN_GLU), lambda i, j: (i, j)),
        scratch_shapes=[pltpu.VMEM((rows.tm, D_MODEL), BF16)],
        compiler_params=_cparams("parallel", "arbitrary"),
        name="pw1_glu",
    )(x, norm_g.reshape(1, D_MODEL), mods, mods, w, w, b2, b2)


def _ln_silu_to_bf16(yc_ref, lng_ref, lnb_ref, y_ref):
    ncb, tm, cw = yc_ref.shape
    rb = min(tm, NORM_ROWS)

    def body(r, carry):
        r0 = pl.multiple_of(r * rb, rb)
        parts = [yc_ref[cb, pl.ds(r0, rb), :] for cb in range(ncb)]
        mu = sum(jnp.sum(p, axis=-1, keepdims=True) for p in parts) * (1.0 / D_MODEL)
        var = sum(jnp.sum(jnp.square(p - mu), axis=-1, keepdims=True) for p in parts) * (1.0 / D_MODEL)
        inv = lax.rsqrt(var + EPS)
        for cb in range(ncb):
            cols = slice(cb * cw, (cb + 1) * cw)
            y = (parts[cb] - mu) * inv * lng_ref[:, cols] + lnb_ref[:, cols]
            y = y * jax.nn.sigmoid(y)
            y_ref[pl.ds(r0, rb), cols] = y.astype(BF16)
        return carry

    lax.fori_loop(0, tm // rb, body, 0)


def _pw2_residual(y_ref, w_ref, b_ref, x_ref, gate_ref, o_ref):
    out = _dot(y_ref[...], w_ref[...]) + b_ref[...]
    o_ref[...] = x_ref[...] + gate_ref[...] * out


def _conv_prompt_kernel(glu_ref, halo_ref, st_ref, wd_ref, bd_ref, lng_ref, lnb_ref,
                        w_ref, b_ref, x_ref, gate_ref, o_ref, win_ref, shf_ref, yc_ref, y_ref, *, tiles_per_seq):
    ncb, win_rows, cw = win_ref.shape
    tm = glu_ref.shape[0]

    @pl.when(pl.program_id(1) == 0)
    def _():
        first = pl.program_id(0) % tiles_per_seq == 0
        for cb in range(ncb):
            cols = slice(cb * cw, (cb + 1) * cw)
            win_ref[cb, 0:CONV_HALO, :] = jnp.where(first, st_ref[:, cols], halo_ref[:, cols])
            win_ref[cb, CONV_HALO:, :] = glu_ref[:, cols]

        base = CONV_HALO - (CONV_WIDTH - 1)

        def conv_cols(cb, carry):
            wd = wd_ref[cb]
            for r in range(1, SUBLANE):
                shf_ref[r - 1, 0:win_rows - SUBLANE, :] = win_ref[cb, r:r + win_rows - SUBLANE, :]
            for rb in range(tm // CONV_ROWS):
                acc = jnp.zeros((CONV_ROWS, cw), F32)
                for w in range(CONV_WIDTH):
                    r = (base + w) % SUBLANE
                    start = rb * CONV_ROWS + (base + w) - r
                    if r == 0:
                        tap = win_ref[cb, pl.ds(start, CONV_ROWS), :]
                    else:
                        tap = shf_ref[r - 1, pl.ds(start, CONV_ROWS), :]
                    acc = acc + tap * wd[w:w + 1, :]
                yc_ref[cb, pl.ds(rb * CONV_ROWS, CONV_ROWS), :] = acc + bd_ref[cb]
            return carry

        lax.fori_loop(0, ncb, conv_cols, 0)
        _ln_silu_to_bf16(yc_ref, lng_ref, lnb_ref, y_ref)

    _pw2_residual(y_ref, w_ref, b_ref, x_ref, gate_ref, o_ref)


def _conv_sample_kernel(full_ref, wd_ref, bd_ref, lng_ref, lnb_ref,
                        w_ref, b_ref, x_ref, gate_ref, o_ref, yc_ref, y_ref):
    ncb, _, cw = yc_ref.shape

    @pl.when(pl.program_id(1) == 0)
    def _():
        for cb in range(ncb):
            cols = slice(cb * cw, (cb + 1) * cw)
            wd = wd_ref[cb]
            acc = jnp.zeros((full_ref.shape[1], cw), F32)
            for w in range(CONV_WIDTH):
                acc = acc + full_ref[w, :, cols] * wd[w:w + 1, :]
            yc_ref[cb] = acc + bd_ref[cb]
        _ln_silu_to_bf16(yc_ref, lng_ref, lnb_ref, y_ref)

    _pw2_residual(y_ref, w_ref, b_ref, x_ref, gate_ref, o_ref)


def _conv_weights(w_dw, b_dw):
    ncb = D_MODEL // CONV_COLS
    wd = jnp.pad(w_dw, ((0, CONV_HALO - CONV_WIDTH), (0, 0)))
    wd = wd.reshape(CONV_HALO, ncb, CONV_COLS).transpose(1, 0, 2)
    return wd, b_dw.reshape(ncb, 1, CONV_COLS)


def _conv_pw2_prompt(rows, glu, state, x, mods, w_dw, b_dw, ln_g, ln_b, w, layer, b):
    ncb = D_MODEL // CONV_COLS
    nj = D_MODEL // TN_CONV
    wd, bd = _conv_weights(w_dw, b_dw)
    halo_per_tile = rows.tm // CONV_HALO
    tps = rows.tiles_per_seq
    const2 = lambda i, j: (0, 0)
    const3 = lambda i, j: (0, 0, 0)
    return pl.pallas_call(
        functools.partial(_conv_prompt_kernel, tiles_per_seq=tps),
        out_shape=jax.ShapeDtypeStruct((rows.m, D_MODEL), F32),
        grid=(rows.m // rows.tm, nj),
        in_specs=[
            pl.BlockSpec((rows.tm, D_MODEL), lambda i, j: (i, 0)),
            pl.BlockSpec((CONV_HALO, D_MODEL), lambda i, j: (jnp.maximum(i * halo_per_tile - 1, 0), 0)),
            pl.BlockSpec((None, CONV_HALO, D_MODEL), lambda i, j: (i // tps, 0, 0)),
            pl.BlockSpec((ncb, CONV_HALO, CONV_COLS), const3),
            pl.BlockSpec((ncb, 1, CONV_COLS), const3),
            pl.BlockSpec((1, D_MODEL), const2),
            pl.BlockSpec((1, D_MODEL), const2),
            pl.BlockSpec((None, D_MODEL, TN_CONV), lambda i, j: (layer, 0, j)),
            pl.BlockSpec((1, TN_CONV), lambda i, j: (0, j)),
            pl.BlockSpec((rows.tm, TN_CONV), lambda i, j: (i, j)),
            _mod_spec(rows, mods, 2, TN_CONV, lambda j: j),
        ],
        out_specs=pl.BlockSpec((rows.tm, TN_CONV), lambda i, j: (i, j)),
        scratch_shapes=[
            pltpu.VMEM((ncb, CONV_HALO + rows.tm, CONV_COLS), F32),
            pltpu.VMEM((SUBLANE - 1, CONV_HALO + rows.tm, CONV_COLS), F32),
            pltpu.VMEM((ncb, rows.tm, CONV_COLS), F32),
            pltpu.VMEM((rows.tm, D_MODEL), BF16),
        ],
        compiler_params=_cparams("parallel", "arbitrary"),
        name="conv_pw2_prompt",
    )(glu, glu, state, wd, bd, ln_g.reshape(1, D_MODEL), ln_b.reshape(1, D_MODEL),
      w, b.reshape(1, D_MODEL), x, mods)


def _conv_pw2_sample(rows, full, x, mods, w_dw, b_dw, ln_g, ln_b, w, layer, b):
    ncb = D_MODEL // CONV_COLS
    nj = D_MODEL // TN_CONV
    wd, bd = _conv_weights(w_dw, b_dw)
    const2 = lambda i, j: (0, 0)
    const3 = lambda i, j: (0, 0, 0)
    return pl.pallas_call(
        _conv_sample_kernel,
        out_shape=jax.ShapeDtypeStruct((rows.m, D_MODEL), F32),
        grid=(rows.m // rows.tm, nj),
        in_specs=[
            pl.BlockSpec((CONV_WIDTH, rows.tm, D_MODEL), lambda i, j: (0, i, 0)),
            pl.BlockSpec((ncb, CONV_HALO, CONV_COLS), const3),
            pl.BlockSpec((ncb, 1, CONV_COLS), const3),
            pl.BlockSpec((1, D_MODEL), const2),
            pl.BlockSpec((1, D_MODEL), const2),
            pl.BlockSpec((None, D_MODEL, TN_CONV), lambda i, j: (layer, 0, j)),
            pl.BlockSpec((1, TN_CONV), lambda i, j: (0, j)),
            pl.BlockSpec((rows.tm, TN_CONV), lambda i, j: (i, j)),
            _mod_spec(rows, mods, 2, TN_CONV, lambda j: j),
        ],
        out_specs=pl.BlockSpec((rows.tm, TN_CONV), lambda i, j: (i, j)),
        scratch_shapes=[
            pltpu.VMEM((ncb, rows.tm, CONV_COLS), F32),
            pltpu.VMEM((rows.tm, D_MODEL), BF16),
        ],
        compiler_params=_cparams("parallel", "arbitrary"),
        name="conv_pw2_sample",
    )(full, wd, bd, ln_g.reshape(1, D_MODEL), ln_b.reshape(1, D_MODEL),
      w, b.reshape(1, D_MODEL), x, mods)


def _mlp_kernel(x_ref, g_ref, sc_ref, sh_ref, gate_ref, w1_ref, w2_ref, o_ref, *rest, split, emit):
    if emit:
        w1b_ref, w2b_ref, h_ref = rest
    else:
        (h_ref,) = rest
    f = pl.program_id(1)

    @pl.when(f == 0)
    def _():
        _norm_mod_to_bf16(x_ref, g_ref, sc_ref, sh_ref, h_ref)

    h = h_ref[...]
    sub = w1_ref.shape[1] // split
    part = None
    for s in range(split):
        cs = slice(s * sub, (s + 1) * sub)
        w1, w2 = w1_ref[:, cs], w2_ref[cs, :]
        if emit:
            w1, w2 = w1.astype(BF16), w2.astype(BF16)
            w1b_ref[:, cs] = w1
            w2b_ref[cs, :] = w2
        hid = _dot(h, w1)
        hid = jnp.square(jnp.maximum(hid, 0.0)).astype(BF16)
        contrib = _dot(hid, w2)
        part = contrib if part is None else part + contrib

    @pl.when(f == 0)
    def _():
        o_ref[...] = part

    @pl.when(f > 0)
    def _():
        o_ref[...] += part

    @pl.when(f == pl.num_programs(1) - 1)
    def _():
        o_ref[...] = x_ref[...] + gate_ref[...] * o_ref[...]


def _mlp(rows, x, mods, norm_g, w1, w2, layer, emit=False):
    d_ff = w1.shape[2]
    full = lambda j: 0
    if emit:
        assert rows.m == rows.tm and w1.dtype == F32 and w2.dtype == F32
        tf, split = TF_MLP_EMIT, 1
    else:
        assert w1.dtype == BF16 and w2.dtype == BF16
        tf, split = TF_MLP, MLP_SPLIT
    out_shape = [jax.ShapeDtypeStruct((rows.m, D_MODEL), F32)]
    out_specs = [pl.BlockSpec((rows.tm, D_MODEL), lambda i, f: (i, 0))]
    if emit:
        out_shape += [jax.ShapeDtypeStruct((D_MODEL, d_ff), BF16), jax.ShapeDtypeStruct((d_ff, D_MODEL), BF16)]
        out_specs += [pl.BlockSpec((D_MODEL, tf), lambda i, f: (0, f)), pl.BlockSpec((tf, D_MODEL), lambda i, f: (f, 0))]
    outs = pl.pallas_call(
        functools.partial(_mlp_kernel, split=split, emit=emit),
        out_shape=out_shape,
        grid=(rows.m // rows.tm, d_ff // tf),
        in_specs=[
            pl.BlockSpec((rows.tm, D_MODEL), lambda i, f: (i, 0)),
            pl.BlockSpec((1, D_MODEL), lambda i, f: (0, 0)),
            _mod_spec(rows, mods, 4, D_MODEL, full),
            _mod_spec(rows, mods, 3, D_MODEL, full),
            _mod_spec(rows, mods, 5, D_MODEL, full),
            pl.BlockSpec((None, D_MODEL, tf), lambda i, f: (layer, 0, f)),
            pl.BlockSpec((None, tf, D_MODEL), lambda i, f: (layer, f, 0)),
        ],
        out_specs=out_specs,
        scratch_shapes=[pltpu.VMEM((rows.tm, D_MODEL), BF16)],
        compiler_params=_cparams("parallel", "arbitrary"),
        name="sqrelu_mlp_emit" if emit else "sqrelu_mlp",
    )(x, norm_g.reshape(1, D_MODEL), mods, mods, mods, w1, w2)
    return outs if emit else outs[0]


def _proj_kernel(*refs, qk_norm, n_out):
    if qk_norm:
        x_ref, g_ref, sc_ref, sh_ref, w_ref, ng_ref, cos_ref, sa_ref, sb_ref = refs[:9]
        rest = refs[9:]
    else:
        x_ref, g_ref, sc_ref, sh_ref, w_ref = refs[:5]
        rest = refs[5:]
    out_refs, h_ref = rest[:n_out], rest[n_out]

    @pl.when(pl.program_id(1) == 0)
    def _():
        _norm_mod_to_bf16(x_ref, g_ref, sc_ref, sh_ref, h_ref)

    acc = _dot(h_ref[...], w_ref[...])
    tn = acc.shape[1]
    if not qk_norm:
        for o_ref in out_refs:
            o_ref[...] = acc.astype(o_ref.dtype)
        return

    cos, sa, sb, ng = cos_ref[...], sa_ref[...], sb_ref[...], ng_ref[...]
    for grp in range(tn // HEAD_DIM):
        cols = slice(grp * HEAD_DIM, (grp + 1) * HEAD_DIM)
        a = acc[:, cols]
        y = a * lax.rsqrt(jnp.mean(a * a, axis=-1, keepdims=True) + EPS) * ng
        y = (y * cos + pltpu.roll(y, HEAD_DIM - ROT_DIM // 2, 1) * sa
             + pltpu.roll(y, ROT_DIM // 2, 1) * sb)
        for o_ref in out_refs:
            if o_ref.shape[1] == tn:
                o_ref[:, cols] = y.astype(o_ref.dtype)
            else:
                o_ref[pl.ds(grp, acc.shape[0], stride=tn // HEAD_DIM), :] = y.astype(o_ref.dtype)


def _proj(rows, x, mods, k_shift, k_scale, norm_g, w, layer, out_dtypes, rope=None, head_g=None,
          head_major_f32=False):
    n = w.shape[2]
    full = lambda j: 0
    qk_norm = rope is not None
    in_specs = [
        pl.BlockSpec((rows.tm, D_MODEL), lambda i, j: (i, 0)),
        pl.BlockSpec((1, D_MODEL), lambda i, j: (0, 0)),
        _mod_spec(rows, mods, k_scale, D_MODEL, full),
        _mod_spec(rows, mods, k_shift, D_MODEL, full),
        pl.BlockSpec((None, D_MODEL, TN_PROJ), lambda i, j: (layer, 0, j)),
    ]
    args = [x, norm_g.reshape(1, D_MODEL), mods, mods, w]
    if qk_norm:
        tps = rows.tiles_per_seq
        in_specs.append(pl.BlockSpec((1, HEAD_DIM), lambda i, j: (0, 0)))
        args.append(head_g.reshape(1, HEAD_DIM))
        for t in rope:
            in_specs.append(pl.BlockSpec((rows.tm, HEAD_DIM), lambda i, j: (i % tps, 0)))
            args.append(t)
    out_shape, out_specs = [], []
    for dt in out_dtypes:
        if head_major_f32 and dt == F32:
            assert qk_norm and n == TN_PROJ
            groups = n // HEAD_DIM
            out_shape.append(jax.ShapeDtypeStruct((rows.m * groups, HEAD_DIM), dt))
            out_specs.append(pl.BlockSpec((rows.tm * groups, HEAD_DIM), lambda i, j: (i, 0)))
        else:
            out_shape.append(jax.ShapeDtypeStruct((rows.m, n), dt))
            out_specs.append(pl.BlockSpec((rows.tm, TN_PROJ), lambda i, j: (i, j)))
    outs = pl.pallas_call(
        functools.partial(_proj_kernel, qk_norm=qk_norm, n_out=len(out_dtypes)),
        out_shape=out_shape,
        grid=(rows.m // rows.tm, n // TN_PROJ),
        in_specs=in_specs,
        out_specs=out_specs,
        scratch_shapes=[pltpu.VMEM((rows.tm, D_MODEL), BF16)],
        compiler_params=_cparams("parallel", "arbitrary"),
        name="norm_proj_rope" if qk_norm else "norm_proj",
    )(*args)
    return outs


def _res_linear_kernel(a_ref, w_ref, x_ref, gate_ref, o_ref):
    out = _dot(a_ref[...].astype(BF16), w_ref[...])
    o_ref[...] = x_ref[...] + gate_ref[...] * out


def _res_linear(rows, a, w, layer, x, mods, k_gate):
    _, kdim, n = w.shape
    return pl.pallas_call(
        _res_linear_kernel,
        out_shape=jax.ShapeDtypeStruct((rows.m, n), F32),
        grid=(rows.m // rows.tm, n // TN_PROJ),
        in_specs=[
            pl.BlockSpec((rows.tm, kdim), lambda i, j: (i, 0)),
            pl.BlockSpec((None, kdim, TN_PROJ), lambda i, j: (layer, 0, j)),
            pl.BlockSpec((rows.tm, TN_PROJ), lambda i, j: (i, j)),
            _mod_spec(rows, mods, k_gate, TN_PROJ, lambda j: j),
        ],
        out_specs=pl.BlockSpec((rows.tm, TN_PROJ), lambda i, j: (i, j)),
        compiler_params=_cparams("parallel", "arbitrary"),
        name="out_proj_residual",
    )(a, w, x, mods)


def _lambda_full(lq1_ref, lk1_ref, lq2_ref, lk2_ref, lam_init):
    d1 = jnp.sum(lq1_ref[...] * lk1_ref[...], axis=-1, keepdims=True)
    d2 = jnp.sum(lq2_ref[...] * lk2_ref[...], axis=-1, keepdims=True)
    return jnp.exp(d1) - jnp.exp(d2) + lam_init


def _subln(o, g, lam_init):
    y = o * lax.rsqrt(jnp.mean(o * o, axis=-1, keepdims=True) + EPS)
    return y * g * (1.0 - lam_init)


def _attn_prompt_kernel(q_ref, k_ref, v_ref, lq1_ref, lk1_ref, lq2_ref, lk2_ref, sg_ref,
                        o_ref, m_ref, l_ref, acc_ref, *, lam_init):
    tq = q_ref.shape[0]
    tk, rs = TK_ATTN, ATTN_SUB_ROWS
    i = pl.program_id(2)
    m_ref[...] = jnp.full(m_ref.shape, NEG_BIG, F32)
    l_ref[...] = jnp.zeros(l_ref.shape, F32)
    acc_ref[...] = jnp.zeros(acc_ref.shape, F32)
    c2 = SCALE * math.log2(math.e)

    def chunk(c, diag):
        k0 = pl.multiple_of(c * tk, tk)
        vc = v_ref[pl.ds(k0, tk), :]
        for comp in range(2):
            cols = slice(comp * HEAD_DIM, (comp + 1) * HEAD_DIM)
            kc = k_ref[pl.ds(k0, tk), cols]
            for r in range(tq // rs):
                if diag is not None and (r + 1) * rs <= diag * tk:
                    continue
                rows = slice(r * rs, (r + 1) * rs)
                s = lax.dot_general(q_ref[rows, cols], kc, (((1,), (1,)), ((), ())),
                                    preferred_element_type=F32)
                if diag is not None:
                    row = lax.broadcasted_iota(jnp.int32, s.shape, 0) + r * rs
                    col = lax.broadcasted_iota(jnp.int32, s.shape, 1) + diag * tk
                    s = jnp.where(col <= row, s, NEG_BIG)
                tiles = [s[:, t * LANE:(t + 1) * LANE] for t in range(tk // LANE)]
                mx = functools.reduce(jnp.maximum, tiles)
                m_old = m_ref[comp, rows, :]
                m_new = jnp.maximum(m_old, jnp.max(mx, axis=-1, keepdims=True))
                alpha = jnp.exp2((m_old - m_new) * c2)
                ps = [jnp.exp2((t - m_new) * c2) for t in tiles]
                psum = functools.reduce(lambda a, b: a + b, ps)
                l_ref[comp, rows, :] = alpha * l_ref[comp, rows, :] + jnp.sum(psum, axis=-1, keepdims=True)
                m_ref[comp, rows, :] = m_new
                pv = _dot(jnp.concatenate([t.astype(BF16) for t in ps], axis=1), vc)
                for t in range(V_DIM // LANE):
                    lc = slice(t * LANE, (t + 1) * LANE)
                    acc_ref[comp, rows, lc] = alpha * acc_ref[comp, rows, lc] + pv[:, lc]

    def full_chunk(c, carry):
        chunk(c, None)
        return carry

    nd = tq // tk
    lax.fori_loop(0, i * nd, full_chunk, 0)
    for d in range(nd):
        chunk(i * nd + d, d)

    lam = _lambda_full(lq1_ref, lk1_ref, lq2_ref, lk2_ref, lam_init)
    outs = []
    for t in range(V_DIM // LANE):
        lc = slice(t * LANE, (t + 1) * LANE)
        outs.append(acc_ref[0, :, lc] / l_ref[0] - lam * (acc_ref[1, :, lc] / l_ref[1]))
    ms = sum(jnp.sum(o * o, axis=-1, keepdims=True) for o in outs) * (1.0 / V_DIM)
    inv = lax.rsqrt(ms + EPS) * (1.0 - lam_init)
    for t in range(V_DIM // LANE):
        lc = slice(t * LANE, (t + 1) * LANE)
        o_ref[:, lc] = (outs[t] * inv * sg_ref[:, lc]).astype(o_ref.dtype)


def _attn_prompt(q, k, v, lam_params, subln_g, lam_init, n_seq, seq):
    nq = seq // TQ_ATTN
    lam_spec = pl.BlockSpec((1, HEAD_DIM), lambda b, h, i: (0, 0))
    return pl.pallas_call(
        functools.partial(_attn_prompt_kernel, lam_init=lam_init),
        out_shape=jax.ShapeDtypeStruct(q.shape, BF16),
        grid=(n_seq, N_HEADS, nq),
        in_specs=[
            pl.BlockSpec((TQ_ATTN, V_DIM), lambda b, h, i: (b * nq + i, h)),
            pl.BlockSpec((seq, V_DIM), lambda b, h, i: (b, h)),
            pl.BlockSpec((seq, V_DIM), lambda b, h, i: (b, h)),
            lam_spec, lam_spec, lam_spec, lam_spec,
            pl.BlockSpec((1, V_DIM), lambda b, h, i: (0, 0)),
        ],
        out_specs=pl.BlockSpec((TQ_ATTN, V_DIM), lambda b, h, i: (b * nq + i, h)),
        scratch_shapes=[
            pltpu.VMEM((2, TQ_ATTN, LANE), F32),
            pltpu.VMEM((2, TQ_ATTN, LANE), F32),
            pltpu.VMEM((2, TQ_ATTN, V_DIM), F32),
        ],
        compiler_params=_cparams("parallel", "parallel", "arbitrary"),
        name="diff_attn_prompt",
    )(q, k, v, *[p.reshape(1, HEAD_DIM) for p in lam_params], subln_g.reshape(1, V_DIM))


def _attn_sample_kernel(pt_ref, q_ref, kn_ref, vn_ref, lq1_ref, lk1_ref, lq2_ref, lk2_ref, sg_ref,
                        *refs, lam_init):
    npg = PAGES_PER_STEP
    k_refs, v_refs = refs[:npg], refs[npg:2 * npg]
    o_ref, m_ref, l_ref, acc_ref = refs[2 * npg:]
    c = pl.program_id(1)
    nr = 2 * N_HEADS
    flat = k_refs[0].shape[0]

    @pl.when(c == 0)
    def _():
        m_ref[...] = jnp.full(m_ref.shape, NEG_BIG, F32)
        l_ref[...] = jnp.zeros(l_ref.shape, F32)
        acc_ref[...] = jnp.zeros(acc_ref.shape, F32)

    q = q_ref[...]
    q_bf = q.astype(BF16)
    row = lax.broadcasted_iota(jnp.int32, (nr, flat), 0)
    lane = lax.broadcasted_iota(jnp.int32, (nr, flat), 1)
    own = (N_HEADS - 1 - jnp.bitwise_and(row, N_HEADS - 1)) * 2 + jnp.right_shift(row, 3)
    valid = jnp.bitwise_and(lane, nr - 1) == own

    def align_to_values(pr):
        blocks = []
        for half in range(2):
            for comp in range(2):
                base = (half * N_HEADS - (N_HEADS - 1) - comp) % LANE
                tiles = [pltpu.roll(pr[comp * N_HEADS:(comp + 1) * N_HEADS, t * LANE:(t + 1) * LANE],
                                    base, 1, stride=1, stride_axis=0) for t in range(flat // LANE)]
                blocks.append(jnp.concatenate(tiles, axis=1))
        return jnp.concatenate(blocks, axis=0)

    m_old = m_ref[...]
    m_new = m_old
    s_pages = []
    for p in range(npg):
        s = lax.dot_general(q_bf, k_refs[p][...].astype(BF16), (((1,), (1,)), ((), ())),
                            preferred_element_type=F32) * SCALE
        s = jnp.where(valid, s, NEG_BIG)
        s_pages.append(s)
        m_new = jnp.maximum(m_new, jnp.max(s, axis=-1, keepdims=True))
    alpha = jnp.exp(m_old - m_new)
    l_new = alpha * l_ref[...]
    pv = jnp.zeros(acc_ref.shape, F32)
    for p in range(npg):
        pr = jnp.exp(s_pages[p] - m_new)
        l_new = l_new + jnp.sum(pr, axis=-1, keepdims=True)
        pv = pv + _dot(align_to_values(pr).astype(BF16), v_refs[p][...].astype(BF16))
    acc_ref[...] = jnp.concatenate([alpha, alpha], axis=0) * acc_ref[...] + pv
    l_ref[...] = l_new
    m_ref[...] = m_new

    @pl.when(c == pl.num_programs(1) - 1)
    def _():
        s_new = jnp.sum(q * kn_ref[...], axis=-1, keepdims=True) * SCALE
        m_fin = jnp.maximum(m_ref[...], s_new)
        a = jnp.exp(m_ref[...] - m_fin)
        p_new = jnp.exp(s_new - m_fin)
        l_fin = a * l_ref[...] + p_new
        two = lambda x: jnp.concatenate([x, x], axis=0)
        w_all = (two(a) * acc_ref[...] + two(p_new) * vn_ref[...]) / two(l_fin)
        lam = _lambda_full(lq1_ref, lk1_ref, lq2_ref, lk2_ref, lam_init)
        halves = [w_all[half * nr:half * nr + N_HEADS] - lam * w_all[half * nr + N_HEADS:(half + 1) * nr]
                  for half in range(2)]
        ms = sum(jnp.sum(d * d, axis=-1, keepdims=True) for d in halves) * (1.0 / V_DIM)
        inv = lax.rsqrt(ms + EPS) * (1.0 - lam_init)
        for half in range(2):
            o_ref[half * N_HEADS:(half + 1) * N_HEADS, :] = (
                halves[half] * inv * sg_ref[:, half * LANE:(half + 1) * LANE])


def _attn_sample(q, k_new, v_new, cache_k, cache_v, page_table, lam_params, subln_g, lam_init):
    n_seq, n_pages = page_table.shape
    n_pool = cache_k.shape[0]
    nr = 2 * N_HEADS
    flat = PAGE_SIZE * nr
    npg = PAGES_PER_STEP
    n_chunks = n_pages // npg

    k_flat = cache_k.reshape(n_pool, flat, HEAD_DIM)
    v_flat = cache_v.reshape(n_pool, PAGE_SIZE, N_HEADS, 2, LANE).transpose(0, 1, 3, 2, 4).reshape(n_pool, flat, LANE)

    def qk_rows(x):
        return x.reshape(n_seq, N_HEADS, 2, HEAD_DIM).transpose(0, 2, 1, 3)[:, :, ::-1].reshape(n_seq, nr, HEAD_DIM)

    v_rows = v_new.reshape(n_seq, N_HEADS, 2, LANE).transpose(0, 2, 1, 3)[:, :, ::-1]
    v_rows = jnp.broadcast_to(v_rows[:, :, None], (n_seq, 2, 2, N_HEADS, LANE)).reshape(n_seq, 2 * nr, LANE)

    row_spec = pl.BlockSpec((None, nr, HEAD_DIM), lambda b, c, pt: (b, 0, 0))
    lam_spec = pl.BlockSpec((1, HEAD_DIM), lambda b, c, pt: (0, 0))

    def page_spec(p):
        return pl.BlockSpec((None, flat, LANE), lambda b, c, pt: (pt[b * n_pages + c * npg + p], 0, 0))

    grid_spec = pltpu.PrefetchScalarGridSpec(
        num_scalar_prefetch=1,
        grid=(n_seq, n_chunks),
        in_specs=[row_spec, row_spec, pl.BlockSpec((None, 2 * nr, LANE), lambda b, c, pt: (b, 0, 0)),
                  lam_spec, lam_spec, lam_spec, lam_spec,
                  pl.BlockSpec((1, V_DIM), lambda b, c, pt: (0, 0))]
                 + [page_spec(p) for p in range(npg)] + [page_spec(p) for p in range(npg)],
        out_specs=row_spec,
        scratch_shapes=[
            pltpu.VMEM((nr, 1), F32),
            pltpu.VMEM((nr, 1), F32),
            pltpu.VMEM((2 * nr, LANE), F32),
        ],
    )
    out = pl.pallas_call(
        functools.partial(_attn_sample_kernel, lam_init=lam_init),
        out_shape=jax.ShapeDtypeStruct((n_seq, nr, LANE), F32),
        grid_spec=grid_spec,
        compiler_params=_cparams("parallel", "arbitrary"),
        name="diff_attn_sample",
    )(page_table.reshape(-1), qk_rows(q), qk_rows(k_new), v_rows,
      *[p.reshape(1, HEAD_DIM) for p in lam_params], subln_g.reshape(1, V_DIM),
      *([k_flat] * npg), *([v_flat] * npg))
    return out.reshape(n_seq, 2, N_HEADS, LANE)[:, :, ::-1].transpose(0, 2, 1, 3).reshape(n_seq, N_HEADS * V_DIM)


def _rope_tables(pos):
    half = ROT_DIM // 2
    inv_freq = jnp.power(ROPE_THETA, -jnp.arange(half, dtype=F32) * 2.0 / ROT_DIM)
    ang = pos.astype(F32)[:, None] * inv_freq[None, :]
    cos, sin = jnp.cos(ang), jnp.sin(ang)
    n = pos.shape[0]
    ones = jnp.ones((n, HEAD_DIM - ROT_DIM), F32)
    zeros_tail = jnp.zeros((n, HEAD_DIM - half), F32)
    cos_t = jnp.concatenate([cos, cos, ones], axis=1)
    sin_a = jnp.concatenate([-sin, zeros_tail], axis=1)
    sin_b = jnp.concatenate([jnp.zeros((n, half), F32), sin, jnp.zeros((n, HEAD_DIM - ROT_DIM), F32)], axis=1)
    return cos_t, sin_a, sin_b


def _lam_init(layer):
    return 0.8 - 0.6 * math.exp(-0.3 * layer)


def kernel(x_prompt, x_sample, state_conv, cache_k, cache_v, page_table, c_prompt, c_sample,
           ada_w, ada_b, norm_mix_g, norm_ff_g, w_ff1, w_ff2,
           conv_w_pw1, conv_b_pw1, conv_w_dw, conv_b_dw, conv_ln_g, conv_ln_b, conv_w_pw2, conv_b_pw2,
           kv_ada_w, kv_ada_b, kv_norm_g, w_k, w_v, k_norm_g,
           w_q, q_norm_g, lambda_q1, lambda_k1, lambda_q2, lambda_k2, subln_g, w_o):
    n_seq, seq, _ = x_prompt.shape
    n_dec = x_sample.shape[0]
    assert x_sample.shape[1] == 1 and n_dec <= SAMPLE_ROWS
    assert seq % TM_PROMPT == 0 and seq % TM_PW1 == 0 and seq % TQ_ATTN == 0
    n_pages = page_table.shape[1]
    assert n_pages % PAGES_PER_STEP == 0
    n_past = n_pages * PAGE_SIZE
    width = N_HEADS * V_DIM

    rows_p = Rows(m=n_seq * seq, tm=TM_PROMPT, tiles_per_seq=seq // TM_PROMPT, per_row=False)
    rows_s = Rows(m=SAMPLE_ROWS, tm=SAMPLE_ROWS, tiles_per_seq=1, per_row=True)
    pad_s = SAMPLE_ROWS - n_dec

    c_all = jnp.concatenate([c_sample, c_prompt,
                             jnp.zeros((SAMPLE_ROWS - n_dec - n_seq, D_MODEL), F32)], axis=0)
    mods_all = _ada(c_all, ada_w, ada_b)
    kv_mods_all = _ada(c_all, kv_ada_w[None], kv_ada_b[None])[0]

    def group_mods(m, rows):
        if rows.per_row:
            return m
        return m[n_dec:n_dec + n_seq].reshape(n_seq, 1, m.shape[-1])

    conv_w_pw1, conv_w_pw2 = conv_w_pw1.astype(BF16), conv_w_pw2.astype(BF16)
    w_k, w_v, w_q, w_o = w_k.astype(BF16), w_v.astype(BF16), w_q.astype(BF16), w_o.astype(BF16)

    def trunk(rows, x, rope, conv_fn, attn_fn, ff_bf16):
        new_glu, ff_out = [], []
        kv_mods = group_mods(kv_mods_all, rows)
        rows_pw1 = rows if rows.per_row else rows._replace(
            tm=TM_PW1, tiles_per_seq=rows.tiles_per_seq * rows.tm // TM_PW1)
        groups = N_HEADS * 2
        k_f32 = v_f32 = k_att = v_att = None
        for l in range(DEPTH):
            mods = group_mods(mods_all[l], rows)
            if l == N_A_LAYERS:
                k_f32, k_bf = _proj(rows, x, kv_mods, 0, 1, kv_norm_g, w_k[None], 0, (F32, BF16),
                                    rope=rope, head_g=k_norm_g, head_major_f32=True)
                v_f32, v_bf = _proj(rows, x, kv_mods, 0, 1, kv_norm_g, w_v[None], 0, (F32, BF16))
                k_att, v_att = (k_f32.reshape(rows.m, groups * HEAD_DIM), v_f32) if rows.per_row else (k_bf, v_bf)
            if l < N_A_LAYERS:
                glu = _pw1_glu(rows_pw1, x, mods, norm_mix_g[l], conv_w_pw1, l, conv_b_pw1[l])
                new_glu.append(glu)
                x = conv_fn(l, glu, x, mods)
            else:
                j = l - N_A_LAYERS
                q_dtype = F32 if rows.per_row else BF16
                (q,) = _proj(rows, x, mods, 0, 1, norm_mix_g[l], w_q, j, (q_dtype,),
                             rope=rope, head_g=q_norm_g[j])
                lam_params = (lambda_q1[j], lambda_k1[j], lambda_q2[j], lambda_k2[j])
                o = attn_fn(q, k_att, v_att, lam_params, subln_g[j], _lam_init(l))
                x = _res_linear(rows, o, w_o, j, x, mods, 2)
            if ff_bf16 is None:
                x, w1b, w2b = _mlp(rows, x, mods, norm_ff_g[l], w_ff1, w_ff2, l, emit=True)
                ff_out.append((w1b, w2b))
            else:
                w1b, w2b = ff_bf16[l]
                x = _mlp(rows, x, mods, norm_ff_g[l], w1b[None], w2b[None], 0)
        return x, new_glu, k_f32, v_f32, ff_out

    rope_s = _rope_tables(jnp.full((SAMPLE_ROWS,), n_past, jnp.int32))
    conv_s_rows = []

    def conv_sample(l, glu, x, mods):
        hist = jnp.pad(state_conv[l].transpose(1, 0, 2), ((0, 0), (0, pad_s), (0, 0)))
        full = jnp.concatenate([hist, glu[None]], axis=0)
        conv_s_rows.append(full[1:, :n_dec].transpose(1, 0, 2))
        return _conv_pw2_sample(rows_s, full, x, mods, conv_w_dw[l], conv_b_dw[l],
                                conv_ln_g[l], conv_ln_b[l], conv_w_pw2, l, conv_b_pw2[l])

    def attn_sample(q, k, v, lam_params, sg, lam_init):
        o = _attn_sample(q[:n_dec], k[:n_dec], v[:n_dec], cache_k, cache_v, page_table,
                         lam_params, sg, lam_init)
        return jnp.pad(o, ((0, pad_s), (0, 0)))

    x_s = jnp.pad(x_sample.reshape(n_dec, D_MODEL), ((0, pad_s), (0, 0)))
    y_s, _, k_s, v_s, ff_bf16 = trunk(rows_s, x_s, rope_s, conv_sample, attn_sample, None)

    rope_p = _rope_tables(jnp.arange(seq))
    state_p = jnp.zeros((n_seq, CONV_HALO, D_MODEL), F32)

    def conv_prompt(l, glu, x, mods):
        return _conv_pw2_prompt(rows_p, glu, state_p, x, mods, conv_w_dw[l], conv_b_dw[l],
                                conv_ln_g[l], conv_ln_b[l], conv_w_pw2, l, conv_b_pw2[l])

    def attn_prompt(q, k, v, lam_params, sg, lam_init):
        return _attn_prompt(q, k, v, lam_params, sg, lam_init, n_seq, seq)

    y_p, glu_p, k_p, v_p, _ = trunk(rows_p, x_prompt.reshape(n_seq * seq, D_MODEL), rope_p,
                                    conv_prompt, attn_prompt, ff_bf16)
    conv_p = jnp.stack([g.reshape(n_seq, seq, D_MODEL)[:, seq - (CONV_WIDTH - 1):] for g in glu_p])

    return (y_p.reshape(n_seq, seq, D_MODEL),
            y_s[:n_dec].reshape(n_dec, 1, D_MODEL),
            conv_p,
            jnp.stack(conv_s_rows),
            k_p.reshape(n_seq, seq, N_HEADS, 2, HEAD_DIM),
            v_p.reshape(n_seq, seq, N_HEADS, V_DIM),
            k_s.reshape(SAMPLE_ROWS, N_HEADS, 2, HEAD_DIM)[:n_dec].reshape(n_dec, 1, N_HEADS, 2, HEAD_DIM),
            v_s[:n_dec].reshape(n_dec, 1, N_HEADS, V_DIM))
```
